```python
import jax, jax.numpy as jnp
from jax import lax
import numpy as np

D_MODEL = 1024
BATCH = 8
SEQ = 4096
DEPTH = 2

N_BRANCH = 4
BRANCH_W = D_MODEL // 4
FOURIER_GROUPS = 4
FOURIER_GW = BRANCH_W // FOURIER_GROUPS
CONV_WIDTH = 31
CONV_PAD = CONV_WIDTH // 2
LN_EPS = 1e-5
HEAD_DIM = 64
HEADS_PER_GROUP = BRANCH_W // HEAD_DIM
DILATED_CFG = ((128, 1), (512, 4), (2048, 16))
N_ATT_GROUPS = len(DILATED_CFG)
ATT_QKV_W = N_ATT_GROUPS * HEADS_PER_GROUP * HEAD_DIM
ROPE_THETA = 10000.0
NEG_BIG = -1e30
POOL_SIZES = (2, 4, 8, 16)
POOL_GROUPS = len(POOL_SIZES)
POOL_GW = BRANCH_W // POOL_GROUPS
NORM_EPS = 1e-6

IN_A = BRANCH_W
IN_B = 2 * BRANCH_W
IN_C = 3 * ATT_QKV_W
IN_D = BRANCH_W
IN_VALUE = IN_A + IN_B + IN_C + IN_D
D_IN = IN_VALUE + N_BRANCH * BRANCH_W
SPLIT_POINTS = (IN_A, IN_A + IN_B, IN_A + IN_B + IN_C, IN_VALUE)

kernel_name = "hybrid_parallel_gated_mixer_encoder"


def rms_norm(x, g):
    xf = x.astype(jnp.float32)
    y = xf * lax.rsqrt(jnp.mean(xf * xf, axis=-1, keepdims=True) + NORM_EPS)
    return (y * g.astype(jnp.float32)).astype(x.dtype)


def layer_norm(x, g, b):
    xf = x.astype(jnp.float32)
    mu = jnp.mean(xf, axis=-1, keepdims=True)
    var = jnp.mean(jnp.square(xf - mu), axis=-1, keepdims=True)
    y = (xf - mu) * lax.rsqrt(var + LN_EPS)
    return (y * g.astype(jnp.float32) + b.astype(jnp.float32)).astype(x.dtype)


def rope_tables(seq):
    inv = 1.0 / (ROPE_THETA ** (jnp.arange(0, HEAD_DIM, 2, dtype=jnp.float32) / HEAD_DIM))
    ang = jnp.arange(seq, dtype=jnp.float32)[:, None] * inv[None, :]
    return jnp.cos(ang), jnp.sin(ang)


def apply_rope(t, cos, sin):
    half = t.shape[-1] // 2
    t1 = t[..., :half].astype(jnp.float32)
    t2 = t[..., half:].astype(jnp.float32)
    c = cos[None, :, None, :]
    s = sin[None, :, None, :]
    return jnp.concatenate([t1 * c - t2 * s, t2 * c + t1 * s], axis=-1).astype(t.dtype)


def fourier_mix(u, w_lin):
    B, S, _ = u.shape
    ug = u.astype(jnp.float32).reshape(B, S, FOURIER_GROUPS, FOURIER_GW)
    f = jnp.fft.fft2(ug, axes=(1, 3), norm="ortho").real
    f = f.astype(u.dtype).reshape(B, S, BRANCH_W)
    return f @ w_lin


def conformer_conv(u2, conv_w, conv_b, ln_g, ln_b, w_pw):
    a, g = jnp.split(u2, 2, axis=-1)
    u = a * jax.nn.sigmoid(g)
    y = lax.conv_general_dilated(
        u, conv_w.astype(u.dtype)[:, None, :], window_strides=(1,),
        padding=[(CONV_PAD, CONV_PAD)], dimension_numbers=("NWC", "WIO", "NWC"),
        feature_group_count=BRANCH_W) + conv_b
    y = jax.nn.silu(layer_norm(y, ln_g, ln_b))
    return y @ w_pw


def band_attention(q, k, v, half):
    N, L, H, dh = q.shape
    blk = half
    nb = -(-L // blk)
    Lp = nb * blk
    pad = Lp - L
    q = jnp.pad(q, ((0, 0), (0, pad), (0, 0), (0, 0)))
    k = jnp.pad(k, ((0, 0), (blk, pad + blk), (0, 0), (0, 0)))
    v = jnp.pad(v, ((0, 0), (blk, pad + blk), (0, 0), (0, 0)))
    qb = q.reshape(N, nb, blk, H, dh)
    kb = k.reshape(N, nb + 2, blk, H, dh)
    vb = v.reshape(N, nb + 2, blk, H, dh)
    kw = jnp.concatenate([kb[:, :-2], kb[:, 1:-1], kb[:, 2:]], axis=2)
    vw = jnp.concatenate([vb[:, :-2], vb[:, 1:-1], vb[:, 2:]], axis=2)
    s = jnp.einsum("nbqhd,nbkhd->nbhqk", qb, kw).astype(jnp.float32) * (HEAD_DIM ** -0.5)
    qi = jnp.arange(nb)[:, None, None] * blk + jnp.arange(blk)[None, :, None]
    kj = (jnp.arange(nb)[:, None, None] - 1) * blk + jnp.arange(3 * blk)[None, None, :]
    valid = (jnp.abs(qi - kj) <= half) & (kj >= 0) & (kj < L)
    s = jnp.where(valid[None, :, None], s, NEG_BIG)
    m = jnp.max(s, axis=-1, keepdims=True)
    p = jnp.exp(s - m)
    den = jnp.sum(p, axis=-1, keepdims=True)
    o = jnp.einsum("nbhqk,nbkhd->nbqhd", (p / den).astype(v.dtype), vw)
    lse = (m + jnp.log(den))[..., 0]
    o = o.reshape(N, Lp, H, dh)[:, :L]
    lse = lse.transpose(0, 1, 3, 2).reshape(N, Lp, H)[:, :L]
    return o, lse


def dilated_window_attention(q, k, v, dil, half):
    B, S, H, dh = q.shape
    L = S // dil

    def to_res(t):
        return t.reshape(B, L, dil, H, dh).transpose(0, 2, 1, 3, 4).reshape(B * dil, L, H, dh)

    o, lse = band_attention(to_res(q), to_res(k), to_res(v), half)
    o = o.reshape(B, dil, L, H, dh).transpose(0, 2, 1, 3, 4).reshape(B, S, H, dh)
    lse = lse.reshape(B, dil, L, H).transpose(0, 2, 1, 3).reshape(B, S, H)
    return o, lse


def dilated_mixture(qkv, cos, sin):
    B, S, _ = qkv.shape
    qkv = qkv.reshape(B, S, 3, N_ATT_GROUPS * HEADS_PER_GROUP, HEAD_DIM)
    q = apply_rope(qkv[:, :, 0], cos, sin).reshape(B, S, N_ATT_GROUPS, HEADS_PER_GROUP, HEAD_DIM)
    k = apply_rope(qkv[:, :, 1], cos, sin).reshape(B, S, N_ATT_GROUPS, HEADS_PER_GROUP, HEAD_DIM)
    v = qkv[:, :, 2].reshape(B, S, N_ATT_GROUPS, HEADS_PER_GROUP, HEAD_DIM)
    outs, lses = [], []
    for g, (window, dil) in enumerate(DILATED_CFG):
        o, l = dilated_window_attention(q[:, :, g], k[:, :, g], v[:, :, g], dil, window // (2 * dil))
        outs.append(o)
        lses.append(l)
    alpha = jax.nn.softmax(jnp.stack(lses, axis=0), axis=0)
    out = outs[0] * alpha[0][..., None].astype(outs[0].dtype)
    for g in range(1, N_ATT_GROUPS):
        out = out + outs[g] * alpha[g][..., None].astype(outs[g].dtype)
    return out.reshape(B, S, BRANCH_W)


def multiscale_pool(u, w_pool, pool_scale):
    B, S, _ = u.shape
    ug = u.astype(jnp.float32).reshape(B, S, POOL_GROUPS, POOL_GW)
    c = lax.cumsum(ug, axis=1)
    c = jnp.pad(c, ((0, 0), (1, 0), (0, 0), (0, 0)))
    pos = jnp.arange(S)
    outs = []
    for gi, size in enumerate(POOL_SIZES):
        lo = jnp.clip(pos - size // 2, 0, S - 1)
        hi = jnp.clip(pos + size - 1 - size // 2, 0, S - 1)
        cg = c[:, :, gi]
        win_sum = cg[:, hi + 1] - cg[:, lo]
        cnt = (hi - lo + 1).astype(jnp.float32)[None, :, None]
        outs.append(win_sum / cnt - ug[:, :, gi])
    pooled = jnp.stack(outs, axis=2).astype(u.dtype)
    y = jnp.einsum("bsgc,gcd->bsgd", pooled, w_pool).reshape(B, S, BRANCH_W)
    return y * pool_scale


def hybrid_layer(x, cos, sin, norm_g, w_in, w_fourier, conv_w, conv_b, conv_ln_g, conv_ln_b,
                 w_pw, w_pool, pool_scale, w_branch, w_gate, b_gate, w_out):
    B, S, _ = x.shape
    h = rms_norm(x, norm_g)
    z = h @ w_in
    u_a, u_b, u_c, u_d, u_gate = jnp.split(z, SPLIT_POINTS, axis=-1)
    gate_paths = jax.nn.silu(u_gate).reshape(B, S, N_BRANCH, BRANCH_W)
    branch_outs = (
        fourier_mix(u_a, w_fourier),
        conformer_conv(u_b, conv_w, conv_b, conv_ln_g, conv_ln_b, w_pw),
        dilated_mixture(u_c, cos, sin),
        multiscale_pool(u_d, w_pool, pool_scale),
    )
    merged = None
    for n, y in enumerate(branch_outs):
        y_n = (y * gate_paths[:, :, n]) @ w_branch[n]
        mg = jax.nn.sigmoid(h @ w_gate[n] + b_gate[n])
        merged = mg * y_n if merged is None else merged + mg * y_n
    return x + merged @ w_out


def setup_inputs(seed: int = 0) -> dict:
    key = jax.random.key(seed)
    ks = jax.random.split(key, 16)
    f32 = jnp.float32

    def nrm(k, shape, fan_in):
        return jax.random.normal(k, shape, f32) * (fan_in ** -0.5)

    return {
        "x": jax.random.normal(ks[0], (BATCH, SEQ, D_MODEL), f32),
        "norm_g": 1.0 + 0.05 * jax.random.normal(ks[1], (DEPTH, D_MODEL), f32),
        "w_in": nrm(ks[2], (DEPTH, D_MODEL, D_IN), D_MODEL),
        "w_fourier": nrm(ks[3], (DEPTH, BRANCH_W, BRANCH_W), BRANCH_W),
        "conv_w": nrm(ks[4], (DEPTH, CONV_WIDTH, BRANCH_W), CONV_WIDTH),
        "conv_b": 0.02 * jax.random.normal(ks[5], (DEPTH, BRANCH_W), f32),
        "conv_ln_g": 1.0 + 0.05 * jax.random.normal(ks[6], (DEPTH, BRANCH_W), f32),
        "conv_ln_b": 0.02 * jax.random.normal(ks[7], (DEPTH, BRANCH_W), f32),
        "w_pw": nrm(ks[8], (DEPTH, BRANCH_W, BRANCH_W), BRANCH_W),
        "w_pool": nrm(ks[9], (DEPTH, POOL_GROUPS, POOL_GW, POOL_GW), POOL_GW),
        "pool_scale": 1.0 + 0.1 * jax.random.normal(ks[10], (DEPTH, BRANCH_W), f32),
        "w_branch": nrm(ks[11], (DEPTH, N_BRANCH, BRANCH_W, D_MODEL), BRANCH_W),
        "w_gate": nrm(ks[12], (DEPTH, N_BRANCH, D_MODEL, D_MODEL), D_MODEL),
        "b_gate": 0.1 * jax.random.normal(ks[13], (DEPTH, N_BRANCH, D_MODEL), f32),
        "w_out": nrm(ks[14], (DEPTH, D_MODEL, D_MODEL), D_MODEL),
        "final_g": 1.0 + 0.05 * jax.random.normal(ks[15], (D_MODEL,), f32),
    }


def reference(x, norm_g, w_in, w_fourier, conv_w, conv_b, conv_ln_g, conv_ln_b, w_pw,
              w_pool, pool_scale, w_branch, w_gate, b_gate, w_out, final_g):
    cos, sin = rope_tables(x.shape[1])
    for l in range(DEPTH):
        x = hybrid_layer(x, cos, sin, norm_g[l], w_in[l], w_fourier[l], conv_w[l], conv_b[l],
                         conv_ln_g[l], conv_ln_b[l], w_pw[l], w_pool[l], pool_scale[l],
                         w_branch[l], w_gate[l], b_gate[l], w_out[l])
    return rms_norm(x, final_g)
```

```python
import functools

import numpy as np
import jax
import jax.numpy as jnp
from jax import lax
from jax.experimental import pallas as pl
from jax.experimental.pallas import tpu as pltpu

F32 = jnp.float32
BF16 = jnp.bfloat16

D_MODEL = 1024
N_BRANCH = 4
BRANCH_W = 256
FOURIER_GW = 64
CONV_WIDTH = 31
CONV_PAD = CONV_WIDTH // 2
LN_EPS = 1e-5
HEAD_DIM = 64
HEADS_PER_GROUP = 4
DILATIONS = (1, 4, 16)
ATT_HALF = 64
ATT_QKV_W = 768
ROPE_THETA = 10000.0
NEG_BIG = -1e30
POOL_HALVES = (1, 2, 4, 8)
POOL_GW = 64
NORM_EPS = 1e-6

COL_A = 0
COL_B = 256
COL_Q = 768
COL_K = COL_Q + ATT_QKV_W
COL_V = COL_K + ATT_QKV_W
COL_D = COL_V + ATT_QKV_W
COL_GATE = COL_D + BRANCH_W
D_IN = COL_GATE + N_BRANCH * BRANCH_W

FFT_R = 64
TM_PROJ = 1024
TM = 512
LOG2E = 1.4426950408889634
LN2 = 0.6931471805599453
ATT_TQ = 128
ATT_TK = ATT_TQ + 2 * ATT_HALF
LOCAL_CH = 128
LOCAL_HALO = 16
VMEM_LIMIT = 56 * 1024 * 1024


def _sigmoid(x):
    return 0.5 * jnp.tanh(0.5 * x) + 0.5


def _rms(x, g):
    ms = jnp.mean(x * x, axis=-1, keepdims=True)
    return x * lax.rsqrt(ms + NORM_EPS) * g


def _const_spec(shape):
    nd = len(shape)
    return pl.BlockSpec(shape, lambda *_: (0,) * nd, pipeline_mode=pl.Buffered(1))


def _proj_kernel(x_ref, g_ref, w_ref, dft_ref, cos_ref, sin_ref,
                 pq_ref, ub_ref, ud_ref, gp_ref,
                 q0_ref, k0_ref, v0_ref, q1_ref, k1_ref, v1_ref, q2_ref, k2_ref, v2_ref,
                 scr_ref):
    tm = x_ref.shape[0]
    h = _rms(x_ref[...], g_ref[...]).astype(BF16)

    def proj(a, b):
        return jnp.dot(h, w_ref[:, a:b], preferred_element_type=F32)

    ua = proj(COL_A, COL_B).astype(BF16)
    pq_ref[...] = jnp.dot(ua, dft_ref[...], preferred_element_type=F32).astype(BF16)
    zb = proj(COL_B, COL_Q)
    ub_ref[...] = zb[:, :BRANCH_W] * _sigmoid(zb[:, BRANCH_W:])
    ud_ref[...] = proj(COL_D, COL_GATE)
    zg = proj(COL_GATE, D_IN)
    gp_ref[...] = (zg * _sigmoid(zg)).astype(BF16)

    cos = cos_ref[...]
    sin = sin_ref[...]
    lane = lax.broadcasted_iota(jnp.int32, cos.shape, 1)
    first_half = (lane & (HEAD_DIM - 1)) < HEAD_DIM // 2

    def rope(t):
        outs = []
        for j in range(ATT_QKV_W // 128):
            c = t[:, 128 * j:128 * (j + 1)]
            rot = jnp.where(first_half, pltpu.roll(c, 128 - HEAD_DIM // 2, 1),
                            pltpu.roll(c, HEAD_DIM // 2, 1))
            outs.append(c * cos + rot * sin)
        return outs

    def split_store(chunks, refs):
        for g, d in enumerate(DILATIONS):
            if d == 1:
                refs[g][0, 0] = jnp.concatenate(chunks[2 * g:2 * g + 2], axis=1).astype(BF16)
            else:
                scr_ref[0] = chunks[2 * g]
                scr_ref[1] = chunks[2 * g + 1]
                rows = tm // d
                for r in range(d):
                    refs[g][0, r] = jnp.concatenate(
                        [scr_ref[0, pl.ds(r, rows, stride=d), :],
                         scr_ref[1, pl.ds(r, rows, stride=d), :]], axis=1).astype(BF16)

    q = [c * (HEAD_DIM ** -0.5 * LOG2E) for c in rope(proj(COL_Q, COL_K))]
    split_store(q, (q0_ref, q1_ref, q2_ref))
    split_store(rope(proj(COL_K, COL_V)), (k0_ref, k1_ref, k2_ref))
    v = proj(COL_V, COL_D)
    split_store([v[:, 128 * j:128 * (j + 1)] for j in range(ATT_QKV_W // 128)],
                (v0_ref, v1_ref, v2_ref))


def _proj_call(x2, g, w_in, dft_c, cos_t, sin_t, B, S):
    T = B * S
    tm = TM_PROJ
    nt = S // tm
    row = lambda i: (i, 0)
    qkv_shapes, qkv_specs = [], []
    for d in DILATIONS:
        for _ in range(3):
            qkv_shapes.append(jax.ShapeDtypeStruct((B, d, S // d, BRANCH_W), BF16))
            qkv_specs.append(pl.BlockSpec((1, d, tm // d, BRANCH_W),
                                          lambda i: (i // nt, 0, i % nt, 0)))
    out_shape = [
        jax.ShapeDtypeStruct((T, 2 * BRANCH_W), BF16),
        jax.ShapeDtypeStruct((T, BRANCH_W), F32),
        jax.ShapeDtypeStruct((T, BRANCH_W), F32),
        jax.ShapeDtypeStruct((T, N_BRANCH * BRANCH_W), BF16),
    ] + qkv_shapes
    out_specs = [
        pl.BlockSpec((tm, 2 * BRANCH_W), row),
        pl.BlockSpec((tm, BRANCH_W), row),
        pl.BlockSpec((tm, BRANCH_W), row),
        pl.BlockSpec((tm, N_BRANCH * BRANCH_W), row),
    ] + qkv_specs
    return pl.pallas_call(
        _proj_kernel,
        grid=(T // tm,),
        in_specs=[
            pl.BlockSpec((tm, D_MODEL), row),
            _const_spec((1, D_MODEL)),
            _const_spec((D_MODEL, D_IN)),
            _const_spec((BRANCH_W, 2 * BRANCH_W)),
            pl.BlockSpec((tm, 128), lambda i: (i % nt, 0)),
            pl.BlockSpec((tm, 128), lambda i: (i % nt, 0)),
        ],
        out_specs=out_specs,
        out_shape=out_shape,
        scratch_shapes=[pltpu.VMEM((2, tm, 128), F32)],
        compiler_params=pltpu.CompilerParams(
            dimension_semantics=("arbitrary",), vmem_limit_bytes=VMEM_LIMIT),
        name="proj",
    )(x2, g, w_in, dft_c, cos_t, sin_t)


def _fft1_kernel(p_ref, q_ref, m_ref, tr_ref, ti_ref, zr_ref, zi_ref):
    pq = jnp.concatenate([p_ref[0], q_ref[0]], axis=0)
    y = jnp.dot(m_ref[...], pq, preferred_element_type=F32)
    yr, yi = y[:FFT_R], y[FFT_R:]
    tr, ti = tr_ref[...], ti_ref[...]
    zr_ref[0] = (yr * tr - yi * ti).astype(BF16)
    zi_ref[0] = (yr * ti + yi * tr).astype(BF16)


def _fft3_kernel(zr_ref, zi_ref, m_ref, o_ref):
    z = jnp.concatenate([zr_ref[0], zi_ref[0]], axis=0)
    o_ref[0] = jnp.dot(m_ref[...], z, preferred_element_type=F32).astype(BF16)


def _fourier_seq_dft(pq, consts, B, S):
    m1, m3, tr, ti = consts
    wide = FFT_R * BRANCH_W
    chunk = wide // 4
    p = pq[:, :BRANCH_W].reshape(B, FFT_R, wide)
    q = pq[:, BRANCH_W:].reshape(B, FFT_R, wide)
    blk = pl.BlockSpec((1, FFT_R, chunk), lambda b, j: (b, 0, j))
    tw = pl.BlockSpec((FFT_R, chunk), lambda b, j: (0, j))
    params = pltpu.CompilerParams(dimension_semantics=("arbitrary", "arbitrary"),
                                  vmem_limit_bytes=VMEM_LIMIT)
    zr, zi = pl.pallas_call(
        _fft1_kernel,
        grid=(B, wide // chunk),
        in_specs=[blk, blk, _const_spec((2 * FFT_R, 2 * FFT_R)), tw, tw],
        out_specs=[blk, blk],
        out_shape=[jax.ShapeDtypeStruct((B, FFT_R, wide), BF16)] * 2,
        compiler_params=params,
        name="fft1",
    )(p, q, m1, tr, ti)

    def swap(z):
        return z.reshape(B, FFT_R, FFT_R, BRANCH_W).transpose(0, 2, 1, 3).reshape(B, FFT_R, wide)

    out = pl.pallas_call(
        _fft3_kernel,
        grid=(B, wide // chunk),
        in_specs=[blk, blk, _const_spec((FFT_R, 2 * FFT_R))],
        out_specs=blk,
        out_shape=jax.ShapeDtypeStruct((B, FFT_R, wide), BF16),
        compiler_params=params,
        name="fft3",
    )(swap(zr), swap(zi), m3)
    return out.reshape(B * S, BRANCH_W)


def _local_kernel(ub_ref, ud_ref, cw_ref, cb_ref, lg_ref, lb_ref, wpw_ref, wpool_ref, ps_ref,
                  yb_ref, yd_ref, padb_ref, padd_ref, *, S):
    CH, HALO = LOCAL_CH, LOCAL_HALO
    W = CH + 2 * HALO
    zeros = jnp.zeros((HALO, BRANCH_W), F32)
    for pad_ref, src_ref in ((padb_ref, ub_ref), (padd_ref, ud_ref)):
        pad_ref[0:HALO, :] = zeros
        pad_ref[HALO + S:HALO + S + HALO, :] = zeros
        pad_ref[HALO:HALO + S, :] = src_ref[0]

    lane = lax.broadcasted_iota(jnp.int32, (CH, BRANCH_W), 1)
    grp = lane >> 6
    half = jnp.where(grp == 0, POOL_HALVES[0],
                     jnp.where(grp == 1, POOL_HALVES[1],
                               jnp.where(grp == 2, POOL_HALVES[2], POOL_HALVES[3])))
    row = lax.broadcasted_iota(jnp.int32, (CH, BRANCH_W), 0)

    def shift_up(x, s):
        return pltpu.roll(x, (W - s) % W, 0)

    def body(c, carry):
        base = pl.multiple_of(c * CH, CH)

        accs = []
        for lanes in (slice(0, 128), slice(128, 256)):
            win = padb_ref[pl.ds(base, W), lanes]
            acc = jnp.zeros((CH, 128), F32)
            for r in range(8):
                wr = win if r == 0 else shift_up(win, r)
                for a in range(W // 8):
                    k = 8 * a + r - (HALO - CONV_PAD)
                    if 0 <= k < CONV_WIDTH:
                        acc = acc + wr[8 * a:8 * a + CH] * cw_ref[pl.ds(k, 1), :][:, lanes]
            accs.append(acc)
        y = jnp.concatenate(accs, axis=1) + cb_ref[...]
        mu = jnp.mean(y, axis=-1, keepdims=True)
        yc = y - mu
        var = jnp.mean(yc * yc, axis=-1, keepdims=True)
        yn = yc * lax.rsqrt(var + LN_EPS) * lg_ref[...] + lb_ref[...]
        sw = (yn * _sigmoid(yn)).astype(BF16)
        yb_ref[0, pl.ds(base, CH), :] = jnp.dot(
            sw, wpw_ref[...], preferred_element_type=F32).astype(BF16)

        mid = slice(HALO, HALO + CH)
        low_grp = lax.broadcasted_iota(jnp.int32, (CH, 128), 1) < POOL_GW
        wd0 = padd_ref[pl.ds(base, W), 0:128]
        a2 = wd0 + shift_up(wd0, -1)
        a4 = shift_up(a2, -1) + shift_up(a2, 1)
        wd1 = padd_ref[pl.ds(base, W), 128:256]
        b2 = wd1 + shift_up(wd1, -1)
        b4 = shift_up(b2, -1) + shift_up(b2, 1)
        b8 = shift_up(b4, -2) + shift_up(b4, 2)
        b16 = shift_up(b8, -4) + shift_up(b8, 4)
        wsum = jnp.concatenate([jnp.where(low_grp, a2[mid], a4[mid]),
                                jnp.where(low_grp, b8[mid], b16[mid])], axis=1)
        u = jnp.concatenate([wd0[mid], wd1[mid]], axis=1)
        pos = base + row
        lo = jnp.maximum(pos - half, 0)
        hi = jnp.minimum(pos + half - 1, S - 1)
        cnt = (hi - lo + 1).astype(F32)
        pooled = (wsum / cnt - u).astype(BF16)
        yd = jnp.dot(pooled, wpool_ref[...], preferred_element_type=F32) * ps_ref[...]
        yd_ref[0, pl.ds(base, CH), :] = yd.astype(BF16)
        return carry

    lax.fori_loop(0, S // CH, body, 0)


def _local_call(ub, ud, conv_w, conv_b, ln_g, ln_b, w_pw, w_pool_bd, pool_scale, B, S):
    seq = pl.BlockSpec((1, S, BRANCH_W), lambda b: (b, 0, 0))
    vec = _const_spec((1, BRANCH_W))
    mat = _const_spec((BRANCH_W, BRANCH_W))
    return pl.pallas_call(
        functools.partial(_local_kernel, S=S),
        grid=(B,),
        in_specs=[seq, seq, _const_spec((CONV_WIDTH, BRANCH_W)), vec, vec, vec, mat, mat, vec],
        out_specs=[seq, seq],
        out_shape=[jax.ShapeDtypeStruct((B, S, BRANCH_W), BF16)] * 2,
        scratch_shapes=[pltpu.VMEM((S + 2 * LOCAL_HALO, BRANCH_W), F32)] * 2,
        compiler_params=pltpu.CompilerParams(
            dimension_semantics=("arbitrary",), vmem_limit_bytes=VMEM_LIMIT),
        name="local",
    )(ub, ud, conv_w, conv_b, ln_g, ln_b, w_pw, w_pool_bd, pool_scale)


def _attn_kernel(q_ref, k_ref, v_ref, bias_ref, o_ref, lse_ref, s_scr, p_scr, *, L):
    TQ, TK, H = ATT_TQ, ATT_TK, HEADS_PER_GROUP
    nt = L // TQ
    n_tiles = q_ref.shape[1] * nt
    assert n_tiles % 2 == 0 and n_tiles >= 4
    q_head = lax.broadcasted_iota(jnp.int32, (TQ, BRANCH_W), 1) >> 6
    v_head = lax.broadcasted_iota(jnp.int32, (TK, BRANCH_W), 1) >> 6

    def tile_pos(idx):
        r = idx // nt
        q0 = pl.multiple_of((idx - r * nt) * TQ, TQ)
        ks = pl.multiple_of(jnp.clip(q0 - ATT_HALF, 0, L - TK), ATT_HALF)
        return r, q0, ks

    def score_stage(idx, slot):
        r, q0, ks = tile_pos(idx)
        q = q_ref[0, r, pl.ds(q0, TQ), :]
        k = k_ref[0, r, pl.ds(ks, TK), :]
        bias = bias_ref[(q0 - ks) // ATT_HALF]
        for h in range(H):
            qh = jnp.where(q_head == h, q, jnp.zeros_like(q))
            s_scr[slot, h] = lax.dot_general(qh, k, (((1,), (1,)), ((), ())),
                                             preferred_element_type=F32) + bias

    def softmax_stage(idx, slot):
        r, q0, _ = tile_pos(idx)
        lb = jnp.zeros((TQ, BRANCH_W), F32)
        for h in range(H):
            s = s_scr[slot, h]
            m = jnp.max(s, axis=-1, keepdims=True)
            p = jnp.exp2(s - m)
            den = jnp.sum(p, axis=-1, keepdims=True)
            p_scr[slot, :, h * TK:(h + 1) * TK] = (p / den).astype(BF16)
            lb = jnp.where(q_head == h, m * LN2 + jnp.log(den), lb)
        lse_ref[0, r, pl.ds(q0, TQ), :] = lb

    def value_stage(idx, slot):
        r, q0, ks = tile_pos(idx)
        v = v_ref[0, r, pl.ds(ks, TK), :]
        vs = jnp.concatenate([jnp.where(v_head == h, v, jnp.zeros_like(v)) for h in range(H)],
                             axis=0)
        o_ref[0, r, pl.ds(q0, TQ), :] = jnp.dot(
            p_scr[slot], vs, preferred_element_type=F32).astype(BF16)

    def step(i, slot):
        value_stage(i - 1, 1 - slot)
        score_stage(i + 1, 1 - slot)
        softmax_stage(i, slot)

    score_stage(0, 0)
    score_stage(1, 1)
    softmax_stage(0, 0)

    def body(j, carry):
        i = 2 * j + 1
        step(i, 1)
        step(i + 1, 0)
        return carry

    lax.fori_loop(0, (n_tiles - 2) // 2, body, 0)
    value_stage(n_tiles - 2, 0)
    softmax_stage(n_tiles - 1, 1)
    value_stage(n_tiles - 1, 1)


def _attn_bias():
    i = np.arange(ATT_TQ)[:, None]
    j = np.arange(ATT_TK)[None, :]
    offs = np.arange(0, ATT_TK - ATT_TQ + 1, ATT_HALF)[:, None, None]
    return jnp.asarray(np.where(np.abs(i + offs - j) <= ATT_HALF, 0.0, NEG_BIG), F32)


def _attn_call(q, k, v, bias, B, d, L):
    blk = pl.BlockSpec((1, d, L, BRANCH_W), lambda b: (b, 0, 0, 0))
    return pl.pallas_call(
        functools.partial(_attn_kernel, L=L),
        grid=(B,),
        in_specs=[blk, blk, blk, _const_spec(bias.shape)],
        out_specs=[blk, blk],
        out_shape=[jax.ShapeDtypeStruct((B, d, L, BRANCH_W), BF16),
                   jax.ShapeDtypeStruct((B, d, L, BRANCH_W), F32)],
        scratch_shapes=[
            pltpu.VMEM((2, HEADS_PER_GROUP, ATT_TQ, ATT_TK), F32),
            pltpu.VMEM((2, ATT_TQ, HEADS_PER_GROUP * ATT_TK), BF16)],
        compiler_params=pltpu.CompilerParams(
            dimension_semantics=("arbitrary",), vmem_limit_bytes=VMEM_LIMIT),
        name=f"attn_d{d}",
    )(q, k, v, bias)


def _merge_kernel(x_ref, g_ref, f_ref, yb_ref, yd_ref,
                  o0_ref, l0_ref, o1_ref, l1_ref, o2_ref, l2_ref, gp_ref,
                  wlin_ref, wg_ref, bg_ref, wb_ref, wo_ref, fg_ref,
                  out_ref, so1_ref, sl1_ref, so2_ref, sl2_ref, *, final):
    x = x_ref[...]
    h = _rms(x, g_ref[...]).astype(BF16)

    def interleave(src_ref, scr_ref, d):
        rows = TM // d
        for r in range(d):
            piece = src_ref[0, r].astype(F32)
            scr_ref[0, pl.ds(r, rows, stride=d), :] = piece[:, :128]
            scr_ref[1, pl.ds(r, rows, stride=d), :] = piece[:, 128:]
        return jnp.concatenate([scr_ref[0], scr_ref[1]], axis=1)

    os_ = [o0_ref[0, 0].astype(F32), interleave(o1_ref, so1_ref, DILATIONS[1]),
           interleave(o2_ref, so2_ref, DILATIONS[2])]
    ls = [l0_ref[0, 0], interleave(l1_ref, sl1_ref, DILATIONS[1]),
          interleave(l2_ref, sl2_ref, DILATIONS[2])]
    m = jnp.maximum(jnp.maximum(ls[0], ls[1]), ls[2])
    es = [jnp.exp(l - m) for l in ls]
    den = es[0] + es[1] + es[2]
    yc = (os_[0] * es[0] + os_[1] * es[1] + os_[2] * es[2]) / den

    ya = jnp.dot(f_ref[...], wlin_ref[...], preferred_element_type=F32)
    ys = (ya, yb_ref[...].astype(F32), yc, yd_ref[...].astype(F32))
    merged = None
    for n in range(N_BRANCH):
        gate_path = gp_ref[:, BRANCH_W * n:BRANCH_W * (n + 1)].astype(F32)
        t = (ys[n] * gate_path).astype(BF16)
        yn = jnp.dot(t, wb_ref[n], preferred_element_type=F32)
        mg = _sigmoid(jnp.dot(h, wg_ref[n], preferred_element_type=F32) + bg_ref[n])
        merged = mg * yn if merged is None else merged + mg * yn
    out = x + jnp.dot(merged.astype(BF16), wo_ref[...], preferred_element_type=F32)
    if final:
        out = _rms(out, fg_ref[...])
    out_ref[...] = out


def _merge_call(x2, g, f, yb, yd, att, gp, w_lin, w_gate, b_gate, w_branch, w_out, final_g,
                B, S, final):
    T = B * S
    nt = S // TM
    row = lambda i: (i, 0)
    att_specs = []
    for d in DILATIONS:
        spec = pl.BlockSpec((1, d, TM // d, BRANCH_W), lambda i: (i // nt, 0, i % nt, 0))
        att_specs += [spec, spec]
    tile = lambda w: pl.BlockSpec((TM, w), row)
    return pl.pallas_call(
        functools.partial(_merge_kernel, final=final),
        grid=(T // TM,),
        in_specs=[tile(D_MODEL), _const_spec((1, D_MODEL)), tile(BRANCH_W), tile(BRANCH_W),
                  tile(BRANCH_W)] + att_specs + [
            tile(N_BRANCH * BRANCH_W),
            _const_spec((BRANCH_W, BRANCH_W)),
            _const_spec((N_BRANCH, D_MODEL, D_MODEL)),
            _const_spec((N_BRANCH, 1, D_MODEL)),
            _const_spec((N_BRANCH, BRANCH_W, D_MODEL)),
            _const_spec((D_MODEL, D_MODEL)),
            _const_spec((1, D_MODEL)),
        ],
        out_specs=tile(D_MODEL),
        out_shape=jax.ShapeDtypeStruct((T, D_MODEL), F32),
        scratch_shapes=[pltpu.VMEM((2, TM, 128), F32)] * 4,
        compiler_params=pltpu.CompilerParams(
            dimension_semantics=("arbitrary",), vmem_limit_bytes=VMEM_LIMIT),
        name="merge",
    )(x2, g, f, yb, yd, *att, gp, w_lin, w_gate, b_gate, w_branch, w_out, final_g)


def _dft_constants():
    n = np.arange(FFT_R)
    ang = 2.0 * np.pi * np.outer(n, n) / FFT_R
    a, b = np.cos(ang), np.sin(ang)
    eye = np.eye(BRANCH_W // FOURIER_GW)
    dft_c = np.concatenate([np.kron(eye, a), np.kron(eye, b)], axis=1)
    scale = 1.0 / np.sqrt(FFT_R * FFT_R * FOURIER_GW)
    m1 = np.block([[a, -b], [-b, -a]]) * scale
    m3 = np.concatenate([a, b], axis=1)
    tw = 2.0 * np.pi * np.outer(n, n) / (FFT_R * FFT_R)
    return tuple(jnp.asarray(a, F32) for a in (dft_c, m1, m3, np.cos(tw), -np.sin(tw)))


def _rope_tables(S):
    inv = 1.0 / (ROPE_THETA ** (jnp.arange(0, HEAD_DIM, 2, dtype=F32) / HEAD_DIM))
    ang = jnp.arange(S, dtype=F32)[:, None] * inv[None, :]
    cos, sin = jnp.cos(ang), jnp.sin(ang)
    cos_t = jnp.concatenate([cos, cos, cos, cos], axis=1)
    sin_t = jnp.concatenate([-sin, sin, -sin, sin], axis=1)
    return cos_t, sin_t


def kernel(x, norm_g, w_in, w_fourier, conv_w, conv_b, conv_ln_g, conv_ln_b, w_pw, w_pool,
           pool_scale, w_branch, w_gate, b_gate, w_out, final_g):
    B, S, D = x.shape
    depth = norm_g.shape[0]
    assert D == D_MODEL and S == FFT_R * FFT_R and S % TM_PROJ == 0 and w_in.shape[-1] == D_IN
    dft_c, m1, m3, tw_r, tw_i = _dft_constants()
    dft_c, m1, m3 = dft_c.astype(BF16), m1.astype(BF16), m3.astype(BF16)
    tr = jnp.repeat(tw_r, BRANCH_W, axis=1)
    ti = jnp.repeat(tw_i, BRANCH_W, axis=1)
    cos_t, sin_t = _rope_tables(S)
    att_bias = _attn_bias()
    row = lambda a: a.reshape(1, -1)

    x2 = x.reshape(B * S, D)
    for l in range(depth):
        pq, ub, ud, gp, *qkv = _proj_call(
            x2, row(norm_g[l]), w_in[l].astype(BF16), dft_c, cos_t, sin_t, B, S)
        f = _fourier_seq_dft(pq, (m1, m3, tr, ti), B, S)
        w_pool_bd = jax.scipy.linalg.block_diag(*[w_pool[l, g] for g in range(w_pool.shape[1])])
        yb, yd = _local_call(
            ub.reshape(B, S, BRANCH_W), ud.reshape(B, S, BRANCH_W), conv_w[l], row(conv_b[l]),
            row(conv_ln_g[l]), row(conv_ln_b[l]), w_pw[l].astype(BF16), w_pool_bd.astype(BF16),
            row(pool_scale[l]), B, S)
        att = []
        for g, d in enumerate(DILATIONS):
            att += _attn_call(*qkv[3 * g:3 * g + 3], att_bias, B, d, S // d)
        x2 = _merge_call(
            x2, row(norm_g[l]), f, yb.reshape(B * S, BRANCH_W), yd.reshape(B * S, BRANCH_W), att,
            gp, w_fourier[l].astype(BF16), w_gate[l].astype(BF16),
            b_gate[l].reshape(N_BRANCH, 1, D_MODEL), w_branch[l].astype(BF16),
            w_out[l].astype(BF16), row(final_g), B, S, final=(l == depth - 1))
    return x2.reshape(B, S, D)
```

```python
import functools

import numpy as np
import jax
import jax.numpy as jnp
from jax import lax
from jax.experimental import pallas as pl
from jax.experimental.pallas import tpu as pltpu

F32 = jnp.float32
BF16 = jnp.bfloat16

D_MODEL = 1024
N_BRANCH = 4
BRANCH_W = 256
FOURIER_GW = 64
CONV_WIDTH = 31
CONV_PAD = CONV_WIDTH // 2
LN_EPS = 1e-5
HEAD_DIM = 64
HEADS_PER_GROUP = 4
DILATIONS = (1, 4, 16)
ATT_HALF = 64
ATT_QKV_W = 768
ROPE_THETA = 10000.0
NEG_BIG = -1e30
POOL_HALVES = (1, 2, 4, 8)
POOL_GW = 64
NORM_EPS = 1e-6

COL_A = 0
COL_B = 256
COL_Q = 768
COL_K = COL_Q + ATT_QKV_W
COL_V = COL_K + ATT_QKV_W
COL_D = COL_V + ATT_QKV_W
COL_GATE = COL_D + BRANCH_W
D_IN = COL_GATE + N_BRANCH * BRANCH_W

FFT_R = 64
FFT_PITCH = 72
TM_PROJ = 1024
TM = 512
LOG2E = 1.4426950408889634
LN2 = 0.6931471805599453
ATT_TQ = 128
ATT_TK = ATT_TQ + 2 * ATT_HALF
LOCAL_CH = 128
LOCAL_HALO = 16
VMEM_LIMIT = 56 * 1024 * 1024


def _sigmoid(x):
    return 0.5 * jnp.tanh(0.5 * x) + 0.5


def _rms(x, g):
    ms = jnp.mean(x * x, axis=-1, keepdims=True)
    return x * lax.rsqrt(ms + NORM_EPS) * g


def _const_spec(shape):
    nd = len(shape)
    return pl.BlockSpec(shape, lambda *_: (0,) * nd, pipeline_mode=pl.Buffered(1))


def _proj_kernel(x_ref, g_ref, w_ref, dft_ref, cos_ref, sin_ref,
                 ps_ref, qs_ref, ub_ref, ud_ref, gp_ref,
                 q0_ref, k0_ref, v0_ref, q1_ref, k1_ref, v1_ref, q2_ref, k2_ref, v2_ref,
                 scr_ref):
    tm = x_ref.shape[0]
    h = _rms(x_ref[...], g_ref[...]).astype(BF16)

    def proj(a, b):
        return jnp.dot(h, w_ref[:, a:b], preferred_element_type=F32)

    ua = proj(COL_A, COL_B).astype(BF16)
    pq = jnp.dot(ua, dft_ref[...], preferred_element_type=F32)
    for dst_ref, col in ((ps_ref, 0), (qs_ref, BRANCH_W)):
        rows = tm // FFT_R
        for n1 in range(rows):
            src = slice(FFT_R * n1, FFT_R * (n1 + 1))
            dst = slice(FFT_PITCH * n1, FFT_PITCH * n1 + FFT_R)
            scr_ref[0, dst, :] = pq[src, col:col + 128]
            scr_ref[1, dst, :] = pq[src, col + 128:col + 256]
        for n2 in range(FFT_R):
            dst_ref[0, n2] = jnp.concatenate(
                [scr_ref[0, pl.ds(n2, rows, stride=FFT_PITCH), :],
                 scr_ref[1, pl.ds(n2, rows, stride=FFT_PITCH), :]], axis=1).astype(BF16)
    zb = proj(COL_B, COL_Q)
    ub_ref[0] = zb[:, :BRANCH_W] * _sigmoid(zb[:, BRANCH_W:])
    ud_ref[0] = proj(COL_D, COL_GATE)
    zg = proj(COL_GATE, D_IN)
    gp_ref[...] = (zg * _sigmoid(zg)).astype(BF16)

    cos = cos_ref[...]
    sin = sin_ref[...]
    lane = lax.broadcasted_iota(jnp.int32, cos.shape, 1)
    first_half = (lane & (HEAD_DIM - 1)) < HEAD_DIM // 2

    def rope(t):
        outs = []
        for j in range(ATT_QKV_W // 128):
            c = t[:, 128 * j:128 * (j + 1)]
            rot = jnp.where(first_half, pltpu.roll(c, 128 - HEAD_DIM // 2, 1),
                            pltpu.roll(c, HEAD_DIM // 2, 1))
            outs.append(c * cos + rot * sin)
        return outs

    def split_store(chunks, refs):
        for g, d in enumerate(DILATIONS):
            if d == 1:
                refs[g][0, 0] = jnp.concatenate(chunks[2 * g:2 * g + 2], axis=1).astype(BF16)
            else:
                scr_ref[0, 0:tm, :] = chunks[2 * g]
                scr_ref[1, 0:tm, :] = chunks[2 * g + 1]
                rows = tm // d
                for r in range(d):
                    refs[g][0, r] = jnp.concatenate(
                        [scr_ref[0, pl.ds(r, rows, stride=d), :],
                         scr_ref[1, pl.ds(r, rows, stride=d), :]], axis=1).astype(BF16)

    q = [c * (HEAD_DIM ** -0.5 * LOG2E) for c in rope(proj(COL_Q, COL_K))]
    split_store(q, (q0_ref, q1_ref, q2_ref))
    split_store(rope(proj(COL_K, COL_V)), (k0_ref, k1_ref, k2_ref))
    v = proj(COL_V, COL_D)
    split_store([v[:, 128 * j:128 * (j + 1)] for j in range(ATT_QKV_W // 128)],
                (v0_ref, v1_ref, v2_ref))


def _proj_call(x2, g, w_in, dft_c, cos_t, sin_t, B, S):
    T = B * S
    tm = TM_PROJ
    nt = S // tm
    row = lambda i: (i, 0)
    qkv_shapes, qkv_specs = [], []
    for d in DILATIONS:
        for _ in range(3):
            qkv_shapes.append(jax.ShapeDtypeStruct((B, d, S // d, BRANCH_W), BF16))
            qkv_specs.append(pl.BlockSpec((1, d, tm // d, BRANCH_W),
                                          lambda i: (i // nt, 0, i % nt, 0)))
    slab_shape = jax.ShapeDtypeStruct((B, FFT_R, S // FFT_R, BRANCH_W), BF16)
    slab_spec = pl.BlockSpec((1, FFT_R, tm // FFT_R, BRANCH_W), lambda i: (i // nt, 0, i % nt, 0))
    seq_spec = pl.BlockSpec((1, tm, BRANCH_W), lambda i: (i // nt, i % nt, 0))
    out_shape = [
        slab_shape, slab_shape,
        jax.ShapeDtypeStruct((B, S, BRANCH_W), F32),
        jax.ShapeDtypeStruct((B, S, BRANCH_W), F32),
        jax.ShapeDtypeStruct((T, N_BRANCH * BRANCH_W), BF16),
    ] + qkv_shapes
    out_specs = [
        slab_spec, slab_spec, seq_spec, seq_spec,
        pl.BlockSpec((tm, N_BRANCH * BRANCH_W), row),
    ] + qkv_specs
    return pl.pallas_call(
        _proj_kernel,
        grid=(T // tm,),
        in_specs=[
            pl.BlockSpec((tm, D_MODEL), row),
            _const_spec((1, D_MODEL)),
            _const_spec((D_MODEL, D_IN)),
            _const_spec((BRANCH_W, 2 * BRANCH_W)),
            pl.BlockSpec((tm, 128), lambda i: (i % nt, 0)),
            pl.BlockSpec((tm, 128), lambda i: (i % nt, 0)),
        ],
        out_specs=out_specs,
        out_shape=out_shape,
        scratch_shapes=[pltpu.VMEM((2, tm // FFT_R * FFT_PITCH, 128), F32)],
        compiler_params=pltpu.CompilerParams(
            dimension_semantics=("arbitrary",), vmem_limit_bytes=VMEM_LIMIT),
        name="proj",
    )(x2, g, w_in, dft_c, cos_t, sin_t)


def _dft_kernel(ps_ref, qs_ref, m1_ref, m3_ref, tr_ref, ti_ref, out_ref, zr_scr, zi_scr):
    R, PITCH = FFT_R, FFT_PITCH
    m1 = m1_ref[...]
    m3 = m3_ref[...]

    def stage1(n2, carry):
        x = jnp.concatenate([ps_ref[0, n2], qs_ref[0, n2]], axis=0)
        y = jnp.dot(m1, x, preferred_element_type=F32)
        tr = tr_ref[n2]
        ti = ti_ref[n2]
        dst = pl.ds(pl.multiple_of(n2 * PITCH, 8), R)
        for half in range(2):
            lanes = slice(128 * half, 128 * (half + 1))
            yr, yi = y[:R, lanes], y[R:, lanes]
            zr_scr[half, dst, :] = yr * tr - yi * ti
            zi_scr[half, dst, :] = yr * ti + yi * tr
        return carry

    lax.fori_loop(0, R, stage1, 0, unroll=4)

    def zero_pad(k2, carry):
        pad = pl.ds(pl.multiple_of(k2 * PITCH + R, 8), PITCH - R)
        out_ref[0, 0, pad, :] = jnp.zeros((PITCH - R, 128), F32)
        out_ref[0, 1, pad, :] = jnp.zeros((PITCH - R, 128), F32)
        return carry

    lax.fori_loop(0, R, zero_pad, 0, unroll=8)

    def stage3(k1, carry):
        rows = pl.ds(k1, R, stride=PITCH)
        z = jnp.concatenate(
            [jnp.concatenate([zr_scr[0, rows, :], zr_scr[1, rows, :]], axis=1),
             jnp.concatenate([zi_scr[0, rows, :], zi_scr[1, rows, :]], axis=1)], axis=0)
        o = jnp.dot(m3, z.astype(BF16), preferred_element_type=F32)
        out_ref[0, 0, rows, :] = o[:, :128]
        out_ref[0, 1, rows, :] = o[:, 128:]
        return carry

    lax.fori_loop(0, R, stage3, 0, unroll=4)


def _dft_call(ps, qs, m1, m3, tr, ti, B, S):
    padded = FFT_R * FFT_PITCH
    slab = pl.BlockSpec((1, FFT_R, FFT_R, BRANCH_W), lambda b: (b, 0, 0, 0))
    return pl.pallas_call(
        _dft_kernel,
        grid=(B,),
        in_specs=[slab, slab, _const_spec((2 * FFT_R, 2 * FFT_R)), _const_spec((FFT_R, 2 * FFT_R)),
                  _const_spec((FFT_R, FFT_R, 128)), _const_spec((FFT_R, FFT_R, 128))],
        out_specs=pl.BlockSpec((1, 2, padded, 128), lambda b: (b, 0, 0, 0)),
        out_shape=jax.ShapeDtypeStruct((B, 2, padded, 128), F32),
        scratch_shapes=[pltpu.VMEM((2, padded, 128), F32)] * 2,
        compiler_params=pltpu.CompilerParams(
            dimension_semantics=("arbitrary",), vmem_limit_bytes=VMEM_LIMIT),
        name="dft",
    )(ps, qs, m1, m3, tr, ti)


def _local_kernel(ub_ref, ud_ref, cw_ref, cb_ref, lg_ref, lb_ref, wpw_ref, wpool_ref, ps_ref,
                  yb_ref, yd_ref, padb_ref, padd_ref, *, S):
    CH, HALO = LOCAL_CH, LOCAL_HALO
    W = CH + 2 * HALO
    zeros = jnp.zeros((HALO, BRANCH_W), F32)
    for pad_ref, src_ref in ((padb_ref, ub_ref), (padd_ref, ud_ref)):
        pad_ref[0:HALO, :] = zeros
        pad_ref[HALO + S:HALO + S + HALO, :] = zeros
        pad_ref[HALO:HALO + S, :] = src_ref[0]

    lane = lax.broadcasted_iota(jnp.int32, (CH, BRANCH_W), 1)
    grp = lane >> 6
    half = jnp.where(grp == 0, POOL_HALVES[0],
                     jnp.where(grp == 1, POOL_HALVES[1],
                               jnp.where(grp == 2, POOL_HALVES[2], POOL_HALVES[3])))
    row = lax.broadcasted_iota(jnp.int32, (CH, BRANCH_W), 0)

    def shift_up(x, s):
        return pltpu.roll(x, (W - s) % W, 0)

    def body(c, carry):
        base = pl.multiple_of(c * CH, CH)

        accs = []
        for lanes in (slice(0, 128), slice(128, 256)):
            win = padb_ref[pl.ds(base, W), lanes]
            acc = jnp.zeros((CH, 128), F32)
            for r in range(8):
                wr = win if r == 0 else shift_up(win, r)
                for a in range(W // 8):
                    k = 8 * a + r - (HALO - CONV_PAD)
                    if 0 <= k < CONV_WIDTH:
                        acc = acc + wr[8 * a:8 * a + CH] * cw_ref[pl.ds(k, 1), :][:, lanes]
            accs.append(acc)
        y = jnp.concatenate(accs, axis=1) + cb_ref[...]
        mu = jnp.mean(y, axis=-1, keepdims=True)
        yc = y - mu
        var = jnp.mean(yc * yc, axis=-1, keepdims=True)
        yn = yc * lax.rsqrt(var + LN_EPS) * lg_ref[...] + lb_ref[...]
        sw = (yn * _sigmoid(yn)).astype(BF16)
        yb_ref[0, pl.ds(base, CH), :] = jnp.dot(
            sw, wpw_ref[...], preferred_element_type=F32).astype(BF16)

        mid = slice(HALO, HALO + CH)
        low_grp = lax.broadcasted_iota(jnp.int32, (CH, 128), 1) < POOL_GW
        wd0 = padd_ref[pl.ds(base, W), 0:128]
        a2 = wd0 + shift_up(wd0, -1)
        a4 = shift_up(a2, -1) + shift_up(a2, 1)
        wd1 = padd_ref[pl.ds(base, W), 128:256]
        b2 = wd1 + shift_up(wd1, -1)
        b4 = shift_up(b2, -1) + shift_up(b2, 1)
        b8 = shift_up(b4, -2) + shift_up(b4, 2)
        b16 = shift_up(b8, -4) + shift_up(b8, 4)
        wsum = jnp.concatenate([jnp.where(low_grp, a2[mid], a4[mid]),
                                jnp.where(low_grp, b8[mid], b16[mid])], axis=1)
        u = jnp.concatenate([wd0[mid], wd1[mid]], axis=1)
        pos = base + row
        lo = jnp.maximum(pos - half, 0)
        hi = jnp.minimum(pos + half - 1, S - 1)
        cnt = (hi - lo + 1).astype(F32)
        pooled = (wsum / cnt - u).astype(BF16)
        yd = jnp.dot(pooled, wpool_ref[...], preferred_element_type=F32) * ps_ref[...]
        yd_ref[0, pl.ds(base, CH), :] = yd.astype(BF16)
        return carry

    lax.fori_loop(0, S // CH, body, 0)


def _local_call(ub, ud, conv_w, conv_b, ln_g, ln_b, w_pw, w_pool_bd, pool_scale, B, S):
    seq = pl.BlockSpec((1, S, BRANCH_W), lambda b: (b, 0, 0))
    vec = _const_spec((1, BRANCH_W))
    mat = _const_spec((BRANCH_W, BRANCH_W))
    return pl.pallas_call(
        functools.partial(_local_kernel, S=S),
        grid=(B,),
        in_specs=[seq, seq, _const_spec((CONV_WIDTH, BRANCH_W)), vec, vec, vec, mat, mat, vec],
        out_specs=[seq, seq],
        out_shape=[jax.ShapeDtypeStruct((B, S, BRANCH_W), BF16)] * 2,
        scratch_shapes=[pltpu.VMEM((S + 2 * LOCAL_HALO, BRANCH_W), F32)] * 2,
        compiler_params=pltpu.CompilerParams(
            dimension_semantics=("arbitrary",), vmem_limit_bytes=VMEM_LIMIT),
        name="local",
    )(ub, ud, conv_w, conv_b, ln_g, ln_b, w_pw, w_pool_bd, pool_scale)


def _attn_kernel(q_ref, k_ref, v_ref, bias_ref, o_ref, lse_ref, s_scr, p_scr, *, L):
    TQ, TK, H = ATT_TQ, ATT_TK, HEADS_PER_GROUP
    nt = L // TQ
    n_tiles = q_ref.shape[1] * nt
    assert n_tiles % 2 == 0 and n_tiles >= 4
    q_head = lax.broadcasted_iota(jnp.int32, (TQ, BRANCH_W), 1) >> 6
    v_head = lax.broadcasted_iota(jnp.int32, (TK, BRANCH_W), 1) >> 6

    def tile_pos(idx):
        r = idx // nt
        q0 = pl.multiple_of((idx - r * nt) * TQ, TQ)
        ks = pl.multiple_of(jnp.clip(q0 - ATT_HALF, 0, L - TK), ATT_HALF)
        return r, q0, ks

    def score_stage(idx, slot):
        r, q0, ks = tile_pos(idx)
        q = q_ref[0, r, pl.ds(q0, TQ), :]
        k = k_ref[0, r, pl.ds(ks, TK), :]
        bias = bias_ref[(q0 - ks) // ATT_HALF]
        for h in range(H):
            qh = jnp.where(q_head == h, q, jnp.zeros_like(q))
            s_scr[slot, h] = lax.dot_general(qh, k, (((1,), (1,)), ((), ())),
                                             preferred_element_type=F32) + bias

    def softmax_stage(idx, slot):
        r, q0, _ = tile_pos(idx)
        lb = jnp.zeros((TQ, BRANCH_W), F32)
        for h in range(H):
            s = s_scr[slot, h]
            m = jnp.max(s, axis=-1, keepdims=True)
            p = jnp.exp2(s - m)
            den = jnp.sum(p, axis=-1, keepdims=True)
            p_scr[slot, :, h * TK:(h + 1) * TK] = (p / den).astype(BF16)
            lb = jnp.where(q_head == h, m * LN2 + jnp.log(den), lb)
        lse_ref[0, r, pl.ds(q0, TQ), :] = lb

    def value_stage(idx, slot):
        r, q0, ks = tile_pos(idx)
        v = v_ref[0, r, pl.ds(ks, TK), :]
        vs = jnp.concatenate([jnp.where(v_head == h, v, jnp.zeros_like(v)) for h in range(H)],
                             axis=0)
        o_ref[0, r, pl.ds(q0, TQ), :] = jnp.dot(
            p_scr[slot], vs, preferred_element_type=F32).astype(BF16)

    def step(i, slot):
        value_stage(i - 1, 1 - slot)
        score_stage(i + 1, 1 - slot)
        softmax_stage(i, slot)

    score_stage(0, 0)
    score_stage(1, 1)
    softmax_stage(0, 0)

    def body(j, carry):
        i = 2 * j + 1
        step(i, 1)
        step(i + 1, 0)
        return carry

    lax.fori_loop(0, (n_tiles - 2) // 2, body, 0)
    value_stage(n_tiles - 2, 0)
    softmax_stage(n_tiles - 1, 1)
    value_stage(n_tiles - 1, 1)


def _attn_bias():
    i = np.arange(ATT_TQ)[:, None]
    j = np.arange(ATT_TK)[None, :]
    offs = np.arange(0, ATT_TK - ATT_TQ + 1, ATT_HALF)[:, None, None]
    return jnp.asarray(np.where(np.abs(i + offs - j) <= ATT_HALF, 0.0, NEG_BIG), F32)


def _attn_call(q, k, v, bias, B, d, L):
    blk = pl.BlockSpec((1, d, L, BRANCH_W), lambda b: (b, 0, 0, 0))
    return pl.pallas_call(
        functools.partial(_attn_kernel, L=L),
        grid=(B,),
        in_specs=[blk, blk, blk, _const_spec(bias.shape)],
        out_specs=[blk, blk],
        out_shape=[jax.ShapeDtypeStruct((B, d, L, BRANCH_W), BF16),
                   jax.ShapeDtypeStruct((B, d, L, BRANCH_W), F32)],
        scratch_shapes=[
            pltpu.VMEM((2, HEADS_PER_GROUP, ATT_TQ, ATT_TK), F32),
            pltpu.VMEM((2, ATT_TQ, HEADS_PER_GROUP * ATT_TK), BF16)],
        compiler_params=pltpu.CompilerParams(
            dimension_semantics=("arbitrary",), vmem_limit_bytes=VMEM_LIMIT),
        name=f"attn_d{d}",
    )(q, k, v, bias)


def _merge_kernel(x_ref, g_ref, f_ref, yb_ref, yd_ref,
                  o0_ref, l0_ref, o1_ref, l1_ref, o2_ref, l2_ref, gp_ref,
                  wlin_ref, wg_ref, bg_ref, wb_ref, wo_ref, fg_ref,
                  out_ref, so1_ref, sl1_ref, so2_ref, sl2_ref, *, final):
    x = x_ref[...]
    h = _rms(x, g_ref[...]).astype(BF16)

    def interleave(src_ref, scr_ref, d):
        rows = TM // d
        for r in range(d):
            piece = src_ref[0, r].astype(F32)
            scr_ref[0, pl.ds(r, rows, stride=d), :] = piece[:, :128]
            scr_ref[1, pl.ds(r, rows, stride=d), :] = piece[:, 128:]
        return jnp.concatenate([scr_ref[0], scr_ref[1]], axis=1)

    os_ = [o0_ref[0, 0].astype(F32), interleave(o1_ref, so1_ref, DILATIONS[1]),
           interleave(o2_ref, so2_ref, DILATIONS[2])]
    ls = [l0_ref[0, 0], interleave(l1_ref, sl1_ref, DILATIONS[1]),
          interleave(l2_ref, sl2_ref, DILATIONS[2])]
    m = jnp.maximum(jnp.maximum(ls[0], ls[1]), ls[2])
    es = [jnp.exp(l - m) for l in ls]
    den = es[0] + es[1] + es[2]
    yc = (os_[0] * es[0] + os_[1] * es[1] + os_[2] * es[2]) / den

    f = jnp.concatenate(
        [jnp.concatenate([f_ref[0, half, FFT_PITCH * j:FFT_PITCH * j + FFT_R, :]
                          for j in range(TM // FFT_R)], axis=0) for half in range(2)],
        axis=1).astype(BF16)
    ya = jnp.dot(f, wlin_ref[...], preferred_element_type=F32)
    ys = (ya, yb_ref[0].astype(F32), yc, yd_ref[0].astype(F32))
    merged = None
    for n in range(N_BRANCH):
        gate_path = gp_ref[:, BRANCH_W * n:BRANCH_W * (n + 1)].astype(F32)
        t = (ys[n] * gate_path).astype(BF16)
        yn = jnp.dot(t, wb_ref[n], preferred_element_type=F32)
        mg = _sigmoid(jnp.dot(h, wg_ref[n], preferred_element_type=F32) + bg_ref[n])
        merged = mg * yn if merged is None else merged + mg * yn
    out = x + jnp.dot(merged.astype(BF16), wo_ref[...], preferred_element_type=F32)
    if final:
        out = _rms(out, fg_ref[...])
    out_ref[...] = out


def _merge_call(x2, g, f, yb, yd, att, gp, w_lin, w_gate, b_gate, w_branch, w_out, final_g,
                B, S, final):
    T = B * S
    nt = S // TM
    row = lambda i: (i, 0)
    att_specs = []
    for d in DILATIONS:
        spec = pl.BlockSpec((1, d, TM // d, BRANCH_W), lambda i: (i // nt, 0, i % nt, 0))
        att_specs += [spec, spec]
    tile = lambda w: pl.BlockSpec((TM, w), row)
    seq = pl.BlockSpec((1, TM, BRANCH_W), lambda i: (i // nt, i % nt, 0))
    split = pl.BlockSpec((1, 2, TM // FFT_R * FFT_PITCH, 128), lambda i: (i // nt, 0, i % nt, 0))
    return pl.pallas_call(
        functools.partial(_merge_kernel, final=final),
        grid=(T // TM,),
        in_specs=[tile(D_MODEL), _const_spec((1, D_MODEL)), split, seq, seq] + att_specs + [
            tile(N_BRANCH * BRANCH_W),
            _const_spec((BRANCH_W, BRANCH_W)),
            _const_spec((N_BRANCH, D_MODEL, D_MODEL)),
            _const_spec((N_BRANCH, 1, D_MODEL)),
            _const_spec((N_BRANCH, BRANCH_W, D_MODEL)),
            _const_spec((D_MODEL, D_MODEL)),
            _const_spec((1, D_MODEL)),
        ],
        out_specs=tile(D_MODEL),
        out_shape=jax.ShapeDtypeStruct((T, D_MODEL), F32),
        scratch_shapes=[pltpu.VMEM((2, TM, 128), F32)] * 4,
        compiler_params=pltpu.CompilerParams(
            dimension_semantics=("arbitrary",), vmem_limit_bytes=VMEM_LIMIT),
        name="merge",
    )(x2, g, f, yb, yd, *att, gp, w_lin, w_gate, b_gate, w_branch, w_out, final_g)


def _dft_constants():
    n = np.arange(FFT_R)
    ang = 2.0 * np.pi * np.outer(n, n) / FFT_R
    a, b = np.cos(ang), np.sin(ang)
    eye = np.eye(BRANCH_W // FOURIER_GW)
    dft_c = np.concatenate([np.kron(eye, a), np.kron(eye, b)], axis=1)
    scale = 1.0 / np.sqrt(FFT_R * FFT_R * FOURIER_GW)
    m1 = np.block([[a, -b], [-b, -a]]) * scale
    m3 = np.concatenate([a, b], axis=1)
    tw = 2.0 * np.pi * np.outer(n, n) / (FFT_R * FFT_R)
    return tuple(jnp.asarray(a, F32) for a in (dft_c, m1, m3, np.cos(tw), -np.sin(tw)))


def _rope_tables(S):
    inv = 1.0 / (ROPE_THETA ** (jnp.arange(0, HEAD_DIM, 2, dtype=F32) / HEAD_DIM))
    ang = jnp.arange(S, dtype=F32)[:, None] * inv[None, :]
    cos, sin = jnp.cos(ang), jnp.sin(ang)
    cos_t = jnp.concatenate([cos, cos, cos, cos], axis=1)
    sin_t = jnp.concatenate([-sin, sin, -sin, sin], axis=1)
    return cos_t, sin_t


def kernel(x, norm_g, w_in, w_fourier, conv_w, conv_b, conv_ln_g, conv_ln_b, w_pw, w_pool,
           pool_scale, w_branch, w_gate, b_gate, w_out, final_g):
    B, S, D = x.shape
    depth = norm_g.shape[0]
    assert D == D_MODEL and S == FFT_R * FFT_R and S % TM_PROJ == 0 and w_in.shape[-1] == D_IN
    dft_c, m1, m3, tw_r, tw_i = _dft_constants()
    dft_c, m1, m3 = dft_c.astype(BF16), m1.astype(BF16), m3.astype(BF16)
    tr = jnp.broadcast_to(tw_r.T[:, :, None], (FFT_R, FFT_R, 128))
    ti = jnp.broadcast_to(tw_i.T[:, :, None], (FFT_R, FFT_R, 128))
    cos_t, sin_t = _rope_tables(S)
    att_bias = _attn_bias()
    row = lambda a: a.reshape(1, -1)

    x2 = x.reshape(B * S, D)
    for l in range(depth):
        ps, qs, ub, ud, gp, *qkv = _proj_call(
            x2, row(norm_g[l]), w_in[l].astype(BF16), dft_c, cos_t, sin_t, B, S)
        f = _dft_call(ps, qs, m1, m3, tr, ti, B, S)
        w_pool_bd = jax.scipy.linalg.block_diag(*[w_pool[l, g] for g in range(w_pool.shape[1])])
        yb, yd = _local_call(
            ub, ud, conv_w[l], row(conv_b[l]),
            row(conv_ln_g[l]), row(conv_ln_b[l]), w_pw[l].astype(BF16), w_pool_bd.astype(BF16),
            row(pool_scale[l]), B, S)
        att = []
        for g, d in enumerate(DILATIONS):
            att += _attn_call(*qkv[3 * g:3 * g + 3], att_bias, B, d, S // d)
        x2 = _merge_call(
            x2, row(norm_g[l]), f, yb, yd, att,
            gp, w_fourier[l].astype(BF16), w_gate[l].astype(BF16),
            b_gate[l].reshape(N_BRANCH, 1, D_MODEL), w_branch[l].astype(BF16),
            w_out[l].astype(BF16), row(final_g), B, S, final=(l == depth - 1))
    return x2.reshape(B, S, D)
```

```python
import functools

import numpy as np
import jax
import jax.numpy as jnp
from jax import lax
from jax.experimental import pallas as pl
from jax.experimental.pallas import tpu as pltpu

F32 = jnp.float32
BF16 = jnp.bfloat16

D_MODEL = 1024
N_BRANCH = 4
BRANCH_W = 256
FOURIER_GW = 64
CONV_WIDTH = 31
CONV_PAD = CONV_WIDTH // 2
LN_EPS = 1e-5
HEAD_DIM = 64
HEADS_PER_GROUP = 4
DILATIONS = (1, 4, 16)
ATT_HALF = 64
ATT_QKV_W = 768
ROPE_THETA = 10000.0
NEG_BIG = -1e30
POOL_HALVES = (1, 2, 4, 8)
POOL_GW = 64
NORM_EPS = 1e-6

COL_A = 0
COL_B = 256
COL_Q = 768
COL_K = COL_Q + ATT_QKV_W
COL_V = COL_K + ATT_QKV_W
COL_D = COL_V + ATT_QKV_W
COL_GATE = COL_D + BRANCH_W
D_IN = COL_GATE + N_BRANCH * BRANCH_W

FFT_R = 64
FFT_PITCH = 72
TM_PROJ = 1024
TM = 512
LOG2E = 1.4426950408889634
LN2 = 0.6931471805599453
ATT_TQ = 128
ATT_TK = ATT_TQ + 2 * ATT_HALF
LOCAL_CH = 128
LOCAL_HALO = 16
VMEM_LIMIT = 56 * 1024 * 1024


def _sigmoid(x):
    return 0.5 * jnp.tanh(0.5 * x) + 0.5


def _rms(x, g):
    ms = jnp.mean(x * x, axis=-1, keepdims=True)
    return x * lax.rsqrt(ms + NORM_EPS) * g


def _const_spec(shape):
    nd = len(shape)
    return pl.BlockSpec(shape, lambda *_: (0,) * nd, pipeline_mode=pl.Buffered(1))


def _proj_kernel(x_ref, g_ref, w_ref, dft_ref, cos_ref, sin_ref,
                 ps_ref, qs_ref, ub_ref, ud_ref, gp_ref,
                 q0_ref, k0_ref, v0_ref, q1_ref, k1_ref, v1_ref, q2_ref, k2_ref, v2_ref,
                 scr_ref):
    tm = x_ref.shape[0]
    h = _rms(x_ref[...], g_ref[...]).astype(BF16)

    def proj(a, b):
        return jnp.dot(h, w_ref[:, a:b], preferred_element_type=F32)

    ua = proj(COL_A, COL_B).astype(BF16)
    pq = jnp.dot(ua, dft_ref[...], preferred_element_type=F32)
    for dst_ref, col in ((ps_ref, 0), (qs_ref, BRANCH_W)):
        rows = tm // FFT_R
        for n1 in range(rows):
            src = slice(FFT_R * n1, FFT_R * (n1 + 1))
            dst = slice(FFT_PITCH * n1, FFT_PITCH * n1 + FFT_R)
            scr_ref[0, dst, :] = pq[src, col:col + 128]
            scr_ref[1, dst, :] = pq[src, col + 128:col + 256]
        for n2 in range(FFT_R):
            dst_ref[0, n2] = jnp.concatenate(
                [scr_ref[0, pl.ds(n2, rows, stride=FFT_PITCH), :],
                 scr_ref[1, pl.ds(n2, rows, stride=FFT_PITCH), :]], axis=1).astype(BF16)
    zb = proj(COL_B, COL_Q)
    ub_ref[0] = zb[:, :BRANCH_W] * _sigmoid(zb[:, BRANCH_W:])
    ud_ref[0] = proj(COL_D, COL_GATE)
    zg = proj(COL_GATE, D_IN)
    gp_ref[...] = (zg * _sigmoid(zg)).astype(BF16)

    cos = cos_ref[...]
    sin = sin_ref[...]
    lane = lax.broadcasted_iota(jnp.int32, cos.shape, 1)
    first_half = (lane & (HEAD_DIM - 1)) < HEAD_DIM // 2

    def rope(t):
        outs = []
        for j in range(ATT_QKV_W // 128):
            c = t[:, 128 * j:128 * (j + 1)]
            rot = jnp.where(first_half, pltpu.roll(c, 128 - HEAD_DIM // 2, 1),
                            pltpu.roll(c, HEAD_DIM // 2, 1))
            outs.append(c * cos + rot * sin)
        return outs

    def split_store(chunks, refs):
        for g, d in enumerate(DILATIONS):
            if d == 1:
                refs[g][0, 0] = jnp.concatenate(chunks[2 * g:2 * g + 2], axis=1).astype(BF16)
            else:
                scr_ref[0, 0:tm, :] = chunks[2 * g]
                scr_ref[1, 0:tm, :] = chunks[2 * g + 1]
                rows = tm // d
                for r in range(d):
                    refs[g][0, r] = jnp.concatenate(
                        [scr_ref[0, pl.ds(r, rows, stride=d), :],
                         scr_ref[1, pl.ds(r, rows, stride=d), :]], axis=1).astype(BF16)

    q = [c * (HEAD_DIM ** -0.5 * LOG2E) for c in rope(proj(COL_Q, COL_K))]
    split_store(q, (q0_ref, q1_ref, q2_ref))
    split_store(rope(proj(COL_K, COL_V)), (k0_ref, k1_ref, k2_ref))
    v = proj(COL_V, COL_D)
    split_store([v[:, 128 * j:128 * (j + 1)] for j in range(ATT_QKV_W // 128)],
                (v0_ref, v1_ref, v2_ref))


def _proj_call(x2, g, w_in, dft_c, cos_t, sin_t, B, S):
    T = B * S
    tm = TM_PROJ
    nt = S // tm
    row = lambda i: (i, 0)
    qkv_shapes, qkv_specs = [], []
    for d in DILATIONS:
        for _ in range(3):
            qkv_shapes.append(jax.ShapeDtypeStruct((B, d, S // d, BRANCH_W), BF16))
            qkv_specs.append(pl.BlockSpec((1, d, tm // d, BRANCH_W),
                                          lambda i: (i // nt, 0, i % nt, 0)))
    slab_shape = jax.ShapeDtypeStruct((B, FFT_R, S // FFT_R, BRANCH_W), BF16)
    slab_spec = pl.BlockSpec((1, FFT_R, tm // FFT_R, BRANCH_W), lambda i: (i // nt, 0, i % nt, 0))
    seq_spec = pl.BlockSpec((1, tm, BRANCH_W), lambda i: (i // nt, i % nt, 0))
    out_shape = [
        slab_shape, slab_shape,
        jax.ShapeDtypeStruct((B, S, BRANCH_W), F32),
        jax.ShapeDtypeStruct((B, S, BRANCH_W), F32),
        jax.ShapeDtypeStruct((T, N_BRANCH * BRANCH_W), BF16),
    ] + qkv_shapes
    out_specs = [
        slab_spec, slab_spec, seq_spec, seq_spec,
        pl.BlockSpec((tm, N_BRANCH * BRANCH_W), row),
    ] + qkv_specs
    return pl.pallas_call(
        _proj_kernel,
        grid=(T // tm,),
        in_specs=[
            pl.BlockSpec((tm, D_MODEL), row),
            _const_spec((1, D_MODEL)),
            _const_spec((D_MODEL, D_IN)),
            _const_spec((BRANCH_W, 2 * BRANCH_W)),
            pl.BlockSpec((tm, 128), lambda i: (i % nt, 0)),
            pl.BlockSpec((tm, 128), lambda i: (i % nt, 0)),
        ],
        out_specs=out_specs,
        out_shape=out_shape,
        scratch_shapes=[pltpu.VMEM((2, tm // FFT_R * FFT_PITCH, 128), F32)],
        compiler_params=pltpu.CompilerParams(
            dimension_semantics=("arbitrary",), vmem_limit_bytes=VMEM_LIMIT),
        name="proj",
    )(x2, g, w_in, dft_c, cos_t, sin_t)


def _dft_kernel(ps_ref, qs_ref, m1_ref, m3_ref, tr_ref, ti_ref, out_ref, zr_scr, zi_scr):
    R, PITCH = FFT_R, FFT_PITCH
    m1 = m1_ref[...]
    m3 = m3_ref[...]

    def stage1(n2, carry):
        x = jnp.concatenate([ps_ref[0, n2], qs_ref[0, n2]], axis=0)
        y = jnp.dot(m1, x, preferred_element_type=F32)
        tr = tr_ref[n2]
        ti = ti_ref[n2]
        dst = pl.ds(pl.multiple_of(n2 * PITCH, 8), R)
        for half in range(2):
            lanes = slice(128 * half, 128 * (half + 1))
            yr, yi = y[:R, lanes], y[R:, lanes]
            zr_scr[half, dst, :] = yr * tr - yi * ti
            zi_scr[half, dst, :] = yr * ti + yi * tr
        return carry

    lax.fori_loop(0, R, stage1, 0, unroll=4)

    def zero_pad(k2, carry):
        pad = pl.ds(pl.multiple_of(k2 * PITCH + R, 8), PITCH - R)
        out_ref[0, 0, pad, :] = jnp.zeros((PITCH - R, 128), F32)
        out_ref[0, 1, pad, :] = jnp.zeros((PITCH - R, 128), F32)
        return carry

    lax.fori_loop(0, R, zero_pad, 0, unroll=8)

    def stage3(k1, carry):
        rows = pl.ds(k1, R, stride=PITCH)
        z = jnp.concatenate(
            [jnp.concatenate([zr_scr[0, rows, :], zr_scr[1, rows, :]], axis=1),
             jnp.concatenate([zi_scr[0, rows, :], zi_scr[1, rows, :]], axis=1)], axis=0)
        o = jnp.dot(m3, z.astype(BF16), preferred_element_type=F32)
        out_ref[0, 0, rows, :] = o[:, :128]
        out_ref[0, 1, rows, :] = o[:, 128:]
        return carry

    lax.fori_loop(0, R, stage3, 0, unroll=4)


def _dft_call(ps, qs, m1, m3, tr, ti, B, S):
    padded = FFT_R * FFT_PITCH
    slab = pl.BlockSpec((1, FFT_R, FFT_R, BRANCH_W), lambda b: (b, 0, 0, 0))
    return pl.pallas_call(
        _dft_kernel,
        grid=(B,),
        in_specs=[slab, slab, _const_spec((2 * FFT_R, 2 * FFT_R)), _const_spec((FFT_R, 2 * FFT_R)),
                  _const_spec((FFT_R, FFT_R, 128)), _const_spec((FFT_R, FFT_R, 128))],
        out_specs=pl.BlockSpec((1, 2, padded, 128), lambda b: (b, 0, 0, 0)),
        out_shape=jax.ShapeDtypeStruct((B, 2, padded, 128), F32),
        scratch_shapes=[pltpu.VMEM((2, padded, 128), F32)] * 2,
        compiler_params=pltpu.CompilerParams(
            dimension_semantics=("arbitrary",), vmem_limit_bytes=VMEM_LIMIT),
        name="dft",
    )(ps, qs, m1, m3, tr, ti)


def _conv_pool_chunk(padb_ref, padd_ref, base, pos0, cw_ref, cb_ref, lg_ref, lb_ref, wpw_ref,
                     wpool_ref, ps_ref, S):
    CH, HALO = LOCAL_CH, LOCAL_HALO
    W = CH + 2 * HALO

    def shift_up(x, s):
        return pltpu.roll(x, (W - s) % W, 0)

    accs = []
    for lanes in (slice(0, 128), slice(128, 256)):
        win = padb_ref[pl.ds(base, W), lanes]
        acc = jnp.zeros((CH, 128), F32)
        for r in range(8):
            wr = win if r == 0 else shift_up(win, r)
            for a in range(W // 8):
                k = 8 * a + r - (HALO - CONV_PAD)
                if 0 <= k < CONV_WIDTH:
                    acc = acc + wr[8 * a:8 * a + CH] * cw_ref[pl.ds(k, 1), :][:, lanes]
        accs.append(acc)
    y = jnp.concatenate(accs, axis=1) + cb_ref[...]
    mu = jnp.mean(y, axis=-1, keepdims=True)
    yc = y - mu
    var = jnp.mean(yc * yc, axis=-1, keepdims=True)
    yn = yc * lax.rsqrt(var + LN_EPS) * lg_ref[...] + lb_ref[...]
    sw = (yn * _sigmoid(yn)).astype(BF16)
    yb = jnp.dot(sw, wpw_ref[...], preferred_element_type=F32)

    mid = slice(HALO, HALO + CH)
    low_grp = lax.broadcasted_iota(jnp.int32, (CH, 128), 1) < POOL_GW
    wd0 = padd_ref[pl.ds(base, W), 0:128]
    a2 = wd0 + shift_up(wd0, -1)
    a4 = shift_up(a2, -1) + shift_up(a2, 1)
    wd1 = padd_ref[pl.ds(base, W), 128:256]
    b2 = wd1 + shift_up(wd1, -1)
    b4 = shift_up(b2, -1) + shift_up(b2, 1)
    b8 = shift_up(b4, -2) + shift_up(b4, 2)
    b16 = shift_up(b8, -4) + shift_up(b8, 4)
    wsum = jnp.concatenate([jnp.where(low_grp, a2[mid], a4[mid]),
                            jnp.where(low_grp, b8[mid], b16[mid])], axis=1)
    u = jnp.concatenate([wd0[mid], wd1[mid]], axis=1)
    grp = lax.broadcasted_iota(jnp.int32, (CH, BRANCH_W), 1) >> 6
    half = jnp.where(grp == 0, POOL_HALVES[0],
                     jnp.where(grp == 1, POOL_HALVES[1],
                               jnp.where(grp == 2, POOL_HALVES[2], POOL_HALVES[3])))
    pos = pos0 + lax.broadcasted_iota(jnp.int32, (CH, BRANCH_W), 0)
    lo = jnp.maximum(pos - half, 0)
    hi = jnp.minimum(pos + half - 1, S - 1)
    cnt = (hi - lo + 1).astype(F32)
    pooled = (wsum / cnt - u).astype(BF16)
    yd = jnp.dot(pooled, wpool_ref[...], preferred_element_type=F32) * ps_ref[...]
    return yb, yd


def _attn_kernel(q_ref, k_ref, v_ref, bias_ref, o_ref, lse_ref, s_scr, p_scr, *, L):
    TQ, TK, H = ATT_TQ, ATT_TK, HEADS_PER_GROUP
    nt = L // TQ
    n_tiles = q_ref.shape[1] * nt
    assert n_tiles % 2 == 0 and n_tiles >= 4
    q_head = lax.broadcasted_iota(jnp.int32, (TQ, BRANCH_W), 1) >> 6
    v_head = lax.broadcasted_iota(jnp.int32, (TK, BRANCH_W), 1) >> 6

    def tile_pos(idx):
        r = idx // nt
        q0 = pl.multiple_of((idx - r * nt) * TQ, TQ)
        ks = pl.multiple_of(jnp.clip(q0 - ATT_HALF, 0, L - TK), ATT_HALF)
        return r, q0, ks

    def score_stage(idx, slot):
        r, q0, ks = tile_pos(idx)
        q = q_ref[0, r, pl.ds(q0, TQ), :]
        k = k_ref[0, r, pl.ds(ks, TK), :]
        bias = bias_ref[(q0 - ks) // ATT_HALF]
        for h in range(H):
            qh = jnp.where(q_head == h, q, jnp.zeros_like(q))
            s_scr[slot, h] = lax.dot_general(qh, k, (((1,), (1,)), ((), ())),
                                             preferred_element_type=F32) + bias

    def softmax_stage(idx, slot):
        r, q0, _ = tile_pos(idx)
        lb = jnp.zeros((TQ, BRANCH_W), F32)
        for h in range(H):
            s = s_scr[slot, h]
            m = jnp.max(s, axis=-1, keepdims=True)
            p = jnp.exp2(s - m)
            den = jnp.sum(p, axis=-1, keepdims=True)
            p_scr[slot, :, h * TK:(h + 1) * TK] = (p / den).astype(BF16)
            lb = jnp.where(q_head == h, m * LN2 + jnp.log(den), lb)
        lse_ref[0, r, pl.ds(q0, TQ), :] = lb

    def value_stage(idx, slot):
        r, q0, ks = tile_pos(idx)
        v = v_ref[0, r, pl.ds(ks, TK), :]
        vs = jnp.concatenate([jnp.where(v_head == h, v, jnp.zeros_like(v)) for h in range(H)],
                             axis=0)
        o_ref[0, r, pl.ds(q0, TQ), :] = jnp.dot(
            p_scr[slot], vs, preferred_element_type=F32).astype(BF16)

    def step(i, slot):
        value_stage(i - 1, 1 - slot)
        score_stage(i + 1, 1 - slot)
        softmax_stage(i, slot)

    score_stage(0, 0)
    score_stage(1, 1)
    softmax_stage(0, 0)

    def body(j, carry):
        i = 2 * j + 1
        step(i, 1)
        step(i + 1, 0)
        return carry

    lax.fori_loop(0, (n_tiles - 2) // 2, body, 0)
    value_stage(n_tiles - 2, 0)
    softmax_stage(n_tiles - 1, 1)
    value_stage(n_tiles - 1, 1)


def _attn_bias():
    i = np.arange(ATT_TQ)[:, None]
    j = np.arange(ATT_TK)[None, :]
    offs = np.arange(0, ATT_TK - ATT_TQ + 1, ATT_HALF)[:, None, None]
    return jnp.asarray(np.where(np.abs(i + offs - j) <= ATT_HALF, 0.0, NEG_BIG), F32)


def _attn_call(q, k, v, bias, B, d, L):
    blk = pl.BlockSpec((1, d, L, BRANCH_W), lambda b: (b, 0, 0, 0))
    return pl.pallas_call(
        functools.partial(_attn_kernel, L=L),
        grid=(B,),
        in_specs=[blk, blk, blk, _const_spec(bias.shape)],
        out_specs=[blk, blk],
        out_shape=[jax.ShapeDtypeStruct((B, d, L, BRANCH_W), BF16),
                   jax.ShapeDtypeStruct((B, d, L, BRANCH_W), F32)],
        scratch_shapes=[
            pltpu.VMEM((2, HEADS_PER_GROUP, ATT_TQ, ATT_TK), F32),
            pltpu.VMEM((2, ATT_TQ, HEADS_PER_GROUP * ATT_TK), BF16)],
        compiler_params=pltpu.CompilerParams(
            dimension_semantics=("arbitrary",), vmem_limit_bytes=VMEM_LIMIT),
        name=f"attn_d{d}",
    )(q, k, v, bias)


def _merge_kernel(x_ref, g_ref, f_ref, ubp_ref, ub_ref, ubn_ref, udp_ref, ud_ref, udn_ref,
                  o0_ref, l0_ref, o1_ref, l1_ref, o2_ref, l2_ref, gp_ref,
                  cw_ref, cb_ref, lg_ref, lb_ref, wpw_ref, wpool_ref, ps_ref,
                  wlin_ref, wg_ref, bg_ref, wb_ref, wo_ref, fg_ref,
                  out_ref, so1_ref, sl1_ref, so2_ref, sl2_ref, padb_ref, padd_ref, *, final, S):
    x = x_ref[...]
    h = _rms(x, g_ref[...]).astype(BF16)

    nt = S // TM
    t = pl.program_id(0) % nt
    for pad_ref, p_ref, c_ref, n_ref in ((padb_ref, ubp_ref, ub_ref, ubn_ref),
                                         (padd_ref, udp_ref, ud_ref, udn_ref)):
        pad_ref[0:LOCAL_HALO, :] = jnp.where(t > 0, p_ref[0], 0.0)
        pad_ref[LOCAL_HALO:LOCAL_HALO + TM, :] = c_ref[0]
        pad_ref[LOCAL_HALO + TM:, :] = jnp.where(t < nt - 1, n_ref[0], 0.0)

    def conv_pool(c):
        return _conv_pool_chunk(padb_ref, padd_ref, LOCAL_CH * c, t * TM + LOCAL_CH * c,
                                cw_ref, cb_ref, lg_ref, lb_ref, wpw_ref, wpool_ref, ps_ref, S)

    def interleave(src_ref, scr_ref, d):
        rows = TM // d
        for r in range(d):
            piece = src_ref[0, r].astype(F32)
            scr_ref[0, pl.ds(r, rows, stride=d), :] = piece[:, :128]
            scr_ref[1, pl.ds(r, rows, stride=d), :] = piece[:, 128:]
        return jnp.concatenate([scr_ref[0], scr_ref[1]], axis=1)

    os_ = [o0_ref[0, 0].astype(F32), interleave(o1_ref, so1_ref, DILATIONS[1]),
           interleave(o2_ref, so2_ref, DILATIONS[2])]
    ls = [l0_ref[0, 0], interleave(l1_ref, sl1_ref, DILATIONS[1]),
          interleave(l2_ref, sl2_ref, DILATIONS[2])]
    m = jnp.maximum(jnp.maximum(ls[0], ls[1]), ls[2])
    es = [jnp.exp(l - m) for l in ls]
    den = es[0] + es[1] + es[2]
    yc = (os_[0] * es[0] + os_[1] * es[1] + os_[2] * es[2]) / den

    f = jnp.concatenate(
        [jnp.concatenate([f_ref[0, half, FFT_PITCH * j:FFT_PITCH * j + FFT_R, :]
                          for j in range(TM // FFT_R)], axis=0) for half in range(2)],
        axis=1).astype(BF16)
    ya = jnp.dot(f, wlin_ref[...], preferred_element_type=F32)

    def branch(n, y):
        gate_path = gp_ref[:, BRANCH_W * n:BRANCH_W * (n + 1)].astype(F32)
        yn = jnp.dot((y * gate_path).astype(BF16), wb_ref[n], preferred_element_type=F32)
        th = jnp.tanh(jnp.dot(h, wg_ref[n], preferred_element_type=F32) + bg_ref[n])
        return th * yn + yn

    n_chunks = TM // LOCAL_CH
    merged = branch(0, ya)
    chunks = [conv_pool(c) for c in range(n_chunks // 2)]
    merged = merged + branch(2, yc)
    chunks += [conv_pool(c) for c in range(n_chunks // 2, n_chunks)]
    merged = merged + branch(1, jnp.concatenate([c[0] for c in chunks], axis=0))
    merged = merged + branch(3, jnp.concatenate([c[1] for c in chunks], axis=0))
    out = x + jnp.dot(merged.astype(BF16), wo_ref[...], preferred_element_type=F32)
    if final:
        out = _rms(out, fg_ref[...])
    out_ref[...] = out


def _merge_call(x2, g, f, ub, ud, att, gp, local_w, w_lin, w_gate, b_gate, w_branch, w_out,
                final_g, B, S, final):
    T = B * S
    nt = S // TM
    row = lambda i: (i, 0)
    att_specs = []
    for d in DILATIONS:
        spec = pl.BlockSpec((1, d, TM // d, BRANCH_W), lambda i: (i // nt, 0, i % nt, 0))
        att_specs += [spec, spec]
    tile = lambda w: pl.BlockSpec((TM, w), row)
    seq = pl.BlockSpec((1, TM, BRANCH_W), lambda i: (i // nt, i % nt, 0))
    hb = TM // LOCAL_HALO
    prev = pl.BlockSpec((1, LOCAL_HALO, BRANCH_W),
                        lambda i: (i // nt, jnp.maximum((i % nt) * hb - 1, 0), 0))
    nxt = pl.BlockSpec((1, LOCAL_HALO, BRANCH_W),
                       lambda i: (i // nt, jnp.minimum((i % nt + 1) * hb, S // LOCAL_HALO - 1), 0))
    vec = _const_spec((1, BRANCH_W))
    mat = _const_spec((BRANCH_W, BRANCH_W))
    split = pl.BlockSpec((1, 2, TM // FFT_R * FFT_PITCH, 128), lambda i: (i // nt, 0, i % nt, 0))
    return pl.pallas_call(
        functools.partial(_merge_kernel, final=final, S=S),
        grid=(T // TM,),
        in_specs=[tile(D_MODEL), _const_spec((1, D_MODEL)), split,
                  prev, seq, nxt, prev, seq, nxt] + att_specs + [
            tile(N_BRANCH * BRANCH_W),
            _const_spec((CONV_WIDTH, BRANCH_W)), vec, vec, vec, mat, mat, vec,
            _const_spec((BRANCH_W, BRANCH_W)),
            _const_spec((N_BRANCH, D_MODEL, D_MODEL)),
            _const_spec((N_BRANCH, 1, D_MODEL)),
            _const_spec((N_BRANCH, BRANCH_W, D_MODEL)),
            _const_spec((D_MODEL, D_MODEL)),
            _const_spec((1, D_MODEL)),
        ],
        out_specs=tile(D_MODEL),
        out_shape=jax.ShapeDtypeStruct((T, D_MODEL), F32),
        scratch_shapes=[pltpu.VMEM((2, TM, 128), F32)] * 4
        + [pltpu.VMEM((TM + 2 * LOCAL_HALO, BRANCH_W), F32)] * 2,
        compiler_params=pltpu.CompilerParams(
            dimension_semantics=("arbitrary",), vmem_limit_bytes=VMEM_LIMIT),
        name="merge",
    )(x2, g, f, ub, ub, ub, ud, ud, ud, *att, gp, *local_w, w_lin, w_gate, b_gate, w_branch,
      w_out, final_g)


def _dft_constants():
    n = np.arange(FFT_R)
    ang = 2.0 * np.pi * np.outer(n, n) / FFT_R
    a, b = np.cos(ang), np.sin(ang)
    eye = np.eye(BRANCH_W // FOURIER_GW)
    dft_c = np.concatenate([np.kron(eye, a), np.kron(eye, b)], axis=1)
    scale = 1.0 / np.sqrt(FFT_R * FFT_R * FOURIER_GW)
    m1 = np.block([[a, -b], [-b, -a]]) * scale
    m3 = np.concatenate([a, b], axis=1)
    tw = 2.0 * np.pi * np.outer(n, n) / (FFT_R * FFT_R)
    return tuple(jnp.asarray(a, F32) for a in (dft_c, m1, m3, np.cos(tw), -np.sin(tw)))


def _rope_tables(S):
    inv = 1.0 / (ROPE_THETA ** (jnp.arange(0, HEAD_DIM, 2, dtype=F32) / HEAD_DIM))
    ang = jnp.arange(S, dtype=F32)[:, None] * inv[None, :]
    cos, sin = jnp.cos(ang), jnp.sin(ang)
    cos_t = jnp.concatenate([cos, cos, cos, cos], axis=1)
    sin_t = jnp.concatenate([-sin, sin, -sin, sin], axis=1)
    return cos_t, sin_t


def kernel(x, norm_g, w_in, w_fourier, conv_w, conv_b, conv_ln_g, conv_ln_b, w_pw, w_pool,
           pool_scale, w_branch, w_gate, b_gate, w_out, final_g):
    B, S, D = x.shape
    depth = norm_g.shape[0]
    assert D == D_MODEL and S == FFT_R * FFT_R and S % TM_PROJ == 0 and w_in.shape[-1] == D_IN
    dft_c, m1, m3, tw_r, tw_i = _dft_constants()
    dft_c, m1, m3 = dft_c.astype(BF16), m1.astype(BF16), m3.astype(BF16)
    tr = jnp.broadcast_to(tw_r.T[:, :, None], (FFT_R, FFT_R, 128))
    ti = jnp.broadcast_to(tw_i.T[:, :, None], (FFT_R, FFT_R, 128))
    cos_t, sin_t = _rope_tables(S)
    att_bias = _attn_bias()
    row = lambda a: a.reshape(1, -1)

    x2 = x.reshape(B * S, D)
    for l in range(depth):
        ps, qs, ub, ud, gp, *qkv = _proj_call(
            x2, row(norm_g[l]), w_in[l].astype(BF16), dft_c, cos_t, sin_t, B, S)
        f = _dft_call(ps, qs, m1, m3, tr, ti, B, S)
        w_pool_bd = jax.scipy.linalg.block_diag(*[w_pool[l, g] for g in range(w_pool.shape[1])])
        local_w = (conv_w[l], row(conv_b[l]), row(conv_ln_g[l]), row(conv_ln_b[l]),
                   w_pw[l].astype(BF16), w_pool_bd.astype(BF16), row(pool_scale[l]))
        att = []
        for g, d in enumerate(DILATIONS):
            att += _attn_call(*qkv[3 * g:3 * g + 3], att_bias, B, d, S // d)
        x2 = _merge_call(
            x2, row(norm_g[l]), f, ub, ud, att,
            gp, local_w, w_fourier[l].astype(BF16), (0.5 * w_gate[l]).astype(BF16),
            0.5 * b_gate[l].reshape(N_BRANCH, 1, D_MODEL), w_branch[l].astype(BF16),
            (0.5 * w_out[l]).astype(BF16), row(final_g), B, S, final=(l == depth - 1))
    return x2.reshape(B, S, D)
```

```python
import functools

import numpy as np
import jax
import jax.numpy as jnp
from jax import lax
from jax.experimental import pallas as pl
from jax.experimental.pallas import tpu as pltpu

F32 = jnp.float32
BF16 = jnp.bfloat16

D_MODEL = 1024
N_BRANCH = 4
BRANCH_W = 256
FOURIER_GW = 64
CONV_WIDTH = 31
CONV_PAD = CONV_WIDTH // 2
LN_EPS = 1e-5
HEAD_DIM = 64
HEADS_PER_GROUP = 4
DILATIONS = (1, 4, 16)
ATT_HALF = 64
ATT_QKV_W = 768
ROPE_THETA = 10000.0
NEG_BIG = -1e30
POOL_HALVES = (1, 2, 4, 8)
POOL_GW = 64
NORM_EPS = 1e-6

COL_A = 0
COL_B = 256
COL_Q = 768
COL_K = COL_Q + ATT_QKV_W
COL_V = COL_K + ATT_QKV_W
COL_D = COL_V + ATT_QKV_W
COL_GATE = COL_D + BRANCH_W
D_IN = COL_GATE + N_BRANCH * BRANCH_W

FFT_R = 64
FFT_PITCH = 72
TM_PROJ = 1024
TM = 512
LOG2E = 1.4426950408889634
LN2 = 0.6931471805599453
ATT_TQ = 128
ATT_TK = ATT_TQ + 2 * ATT_HALF
ATT_STAGE_GAP = 4
ATT_SLOTS = 2 * ATT_STAGE_GAP
LOCAL_CH = 128
LOCAL_HALO = 16
VMEM_LIMIT = 56 * 1024 * 1024


def _sigmoid(x):
    return 0.5 * jnp.tanh(0.5 * x) + 0.5


def _rms(x, g):
    ms = jnp.mean(x * x, axis=-1, keepdims=True)
    return x * lax.rsqrt(ms + NORM_EPS) * g


def _const_spec(shape):
    nd = len(shape)
    return pl.BlockSpec(shape, lambda *_: (0,) * nd, pipeline_mode=pl.Buffered(1))


def _proj_kernel(x_ref, g_ref, w_ref, dft_ref, cos_ref, sin_ref,
                 ps_ref, qs_ref, ub_ref, ud_ref, gp_ref,
                 q0_ref, k0_ref, v0_ref, q1_ref, k1_ref, v1_ref, q2_ref, k2_ref, v2_ref,
                 scr_ref):
    tm = x_ref.shape[0]
    h = _rms(x_ref[...], g_ref[...]).astype(BF16)

    def proj(a, b):
        return jnp.dot(h, w_ref[:, a:b], preferred_element_type=F32)

    ua = proj(COL_A, COL_B).astype(BF16)
    pq = jnp.dot(ua, dft_ref[...], preferred_element_type=F32)
    for dst_ref, col in ((ps_ref, 0), (qs_ref, BRANCH_W)):
        rows = tm // FFT_R
        for n1 in range(rows):
            src = slice(FFT_R * n1, FFT_R * (n1 + 1))
            dst = slice(FFT_PITCH * n1, FFT_PITCH * n1 + FFT_R)
            scr_ref[0, dst, :] = pq[src, col:col + 128]
            scr_ref[1, dst, :] = pq[src, col + 128:col + 256]
        for n2 in range(FFT_R):
            dst_ref[0, n2] = jnp.concatenate(
                [scr_ref[0, pl.ds(n2, rows, stride=FFT_PITCH), :],
                 scr_ref[1, pl.ds(n2, rows, stride=FFT_PITCH), :]], axis=1).astype(BF16)
    zb = proj(COL_B, COL_Q)
    ub_ref[0] = zb[:, :BRANCH_W] * _sigmoid(zb[:, BRANCH_W:])
    ud_ref[0] = proj(COL_D, COL_GATE)
    zg = proj(COL_GATE, D_IN)
    gp_ref[...] = (zg * _sigmoid(zg)).astype(BF16)

    cos = cos_ref[...]
    sin = sin_ref[...]
    lane = lax.broadcasted_iota(jnp.int32, cos.shape, 1)
    first_half = (lane & (HEAD_DIM - 1)) < HEAD_DIM // 2

    def rope(t):
        outs = []
        for j in range(ATT_QKV_W // 128):
            c = t[:, 128 * j:128 * (j + 1)]
            rot = jnp.where(first_half, pltpu.roll(c, 128 - HEAD_DIM // 2, 1),
                            pltpu.roll(c, HEAD_DIM // 2, 1))
            outs.append(c * cos + rot * sin)
        return outs

    def split_store(chunks, refs):
        for g, d in enumerate(DILATIONS):
            if d == 1:
                refs[g][0, 0] = jnp.concatenate(chunks[2 * g:2 * g + 2], axis=1).astype(BF16)
            else:
                scr_ref[0, 0:tm, :] = chunks[2 * g]
                scr_ref[1, 0:tm, :] = chunks[2 * g + 1]
                rows = tm // d
                for r in range(d):
                    refs[g][0, r] = jnp.concatenate(
                        [scr_ref[0, pl.ds(r, rows, stride=d), :],
                         scr_ref[1, pl.ds(r, rows, stride=d), :]], axis=1).astype(BF16)

    q = [c * (HEAD_DIM ** -0.5 * LOG2E) for c in rope(proj(COL_Q, COL_K))]
    split_store(q, (q0_ref, q1_ref, q2_ref))
    split_store(rope(proj(COL_K, COL_V)), (k0_ref, k1_ref, k2_ref))
    v = proj(COL_V, COL_D)
    split_store([v[:, 128 * j:128 * (j + 1)] for j in range(ATT_QKV_W // 128)],
                (v0_ref, v1_ref, v2_ref))


def _proj_call(x2, g, w_in, dft_c, cos_t, sin_t, B, S):
    T = B * S
    tm = TM_PROJ
    nt = S // tm
    row = lambda i: (i, 0)
    qkv_shapes, qkv_specs = [], []
    for d in DILATIONS:
        for _ in range(3):
            qkv_shapes.append(jax.ShapeDtypeStruct((B, d, S // d, BRANCH_W), BF16))
            qkv_specs.append(pl.BlockSpec((1, d, tm // d, BRANCH_W),
                                          lambda i: (i // nt, 0, i % nt, 0)))
    slab_shape = jax.ShapeDtypeStruct((B, FFT_R, S // FFT_R, BRANCH_W), BF16)
    slab_spec = pl.BlockSpec((1, FFT_R, tm // FFT_R, BRANCH_W), lambda i: (i // nt, 0, i % nt, 0))
    seq_spec = pl.BlockSpec((1, tm, BRANCH_W), lambda i: (i // nt, i % nt, 0))
    out_shape = [
        slab_shape, slab_shape,
        jax.ShapeDtypeStruct((B, S, BRANCH_W), F32),
        jax.ShapeDtypeStruct((B, S, BRANCH_W), F32),
        jax.ShapeDtypeStruct((T, N_BRANCH * BRANCH_W), BF16),
    ] + qkv_shapes
    out_specs = [
        slab_spec, slab_spec, seq_spec, seq_spec,
        pl.BlockSpec((tm, N_BRANCH * BRANCH_W), row),
    ] + qkv_specs
    return pl.pallas_call(
        _proj_kernel,
        grid=(T // tm,),
        in_specs=[
            pl.BlockSpec((tm, D_MODEL), row),
            _const_spec((1, D_MODEL)),
            _const_spec((D_MODEL, D_IN)),
            _const_spec((BRANCH_W, 2 * BRANCH_W)),
            pl.BlockSpec((tm, 128), lambda i: (i % nt, 0)),
            pl.BlockSpec((tm, 128), lambda i: (i % nt, 0)),
        ],
        out_specs=out_specs,
        out_shape=out_shape,
        scratch_shapes=[pltpu.VMEM((2, tm // FFT_R * FFT_PITCH, 128), F32)],
        compiler_params=pltpu.CompilerParams(
            dimension_semantics=("arbitrary",), vmem_limit_bytes=VMEM_LIMIT),
        name="proj",
    )(x2, g, w_in, dft_c, cos_t, sin_t)


def _dft_kernel(ps_ref, qs_ref, m1_ref, m3_ref, tr_ref, ti_ref, out_ref, zr_scr, zi_scr):
    R, PITCH = FFT_R, FFT_PITCH
    m1 = m1_ref[...]
    m3 = m3_ref[...]

    def stage1(n2, carry):
        x = jnp.concatenate([ps_ref[0, n2], qs_ref[0, n2]], axis=0)
        y = jnp.dot(m1, x, preferred_element_type=F32)
        tr = tr_ref[n2]
        ti = ti_ref[n2]
        dst = pl.ds(pl.multiple_of(n2 * PITCH, 8), R)
        for half in range(2):
            lanes = slice(128 * half, 128 * (half + 1))
            yr, yi = y[:R, lanes], y[R:, lanes]
            zr_scr[half, dst, :] = yr * tr - yi * ti
            zi_scr[half, dst, :] = yr * ti + yi * tr
        return carry

    lax.fori_loop(0, R, stage1, 0, unroll=16)

    def zero_pad(k2, carry):
        pad = pl.ds(pl.multiple_of(k2 * PITCH + R, 8), PITCH - R)
        out_ref[0, 0, pad, :] = jnp.zeros((PITCH - R, 128), F32)
        out_ref[0, 1, pad, :] = jnp.zeros((PITCH - R, 128), F32)
        return carry

    lax.fori_loop(0, R, zero_pad, 0, unroll=8)

    def stage3(k1, carry):
        rows = pl.ds(k1, R, stride=PITCH)
        z = jnp.concatenate(
            [jnp.concatenate([zr_scr[0, rows, :], zr_scr[1, rows, :]], axis=1),
             jnp.concatenate([zi_scr[0, rows, :], zi_scr[1, rows, :]], axis=1)], axis=0)
        o = jnp.dot(m3, z.astype(BF16), preferred_element_type=F32)
        out_ref[0, 0, rows, :] = o[:, :128]
        out_ref[0, 1, rows, :] = o[:, 128:]
        return carry

    lax.fori_loop(0, R, stage3, 0, unroll=16)


def _dft_call(ps, qs, m1, m3, tr, ti, B, S):
    padded = FFT_R * FFT_PITCH
    slab = pl.BlockSpec((1, FFT_R, FFT_R, BRANCH_W), lambda b: (b, 0, 0, 0))
    return pl.pallas_call(
        _dft_kernel,
        grid=(B,),
        in_specs=[slab, slab, _const_spec((2 * FFT_R, 2 * FFT_R)), _const_spec((FFT_R, 2 * FFT_R)),
                  _const_spec((FFT_R, FFT_R, 128)), _const_spec((FFT_R, FFT_R, 128))],
        out_specs=pl.BlockSpec((1, 2, padded, 128), lambda b: (b, 0, 0, 0)),
        out_shape=jax.ShapeDtypeStruct((B, 2, padded, 128), F32),
        scratch_shapes=[pltpu.VMEM((2, padded, 128), F32)] * 2,
        compiler_params=pltpu.CompilerParams(
            dimension_semantics=("arbitrary",), vmem_limit_bytes=VMEM_LIMIT),
        name="dft",
    )(ps, qs, m1, m3, tr, ti)


def _conv_pool_chunk(padb_ref, padd_ref, base, pos0, cw_ref, cb_ref, lg_ref, lb_ref, wpw_ref,
                     wpool_ref, ps_ref, S):
    CH, HALO = LOCAL_CH, LOCAL_HALO
    W = CH + 2 * HALO

    def shift_up(x, s):
        return pltpu.roll(x, (W - s) % W, 0)

    accs = []
    for lanes in (slice(0, 128), slice(128, 256)):
        win = padb_ref[pl.ds(base, W), lanes]
        acc = jnp.zeros((CH, 128), F32)
        for r in range(8):
            wr = win if r == 0 else shift_up(win, r)
            for a in range(W // 8):
                k = 8 * a + r - (HALO - CONV_PAD)
                if 0 <= k < CONV_WIDTH:
                    acc = acc + wr[8 * a:8 * a + CH] * cw_ref[pl.ds(k, 1), :][:, lanes]
        accs.append(acc)
    y = jnp.concatenate(accs, axis=1) + cb_ref[...]
    mu = jnp.mean(y, axis=-1, keepdims=True)
    yc = y - mu
    var = jnp.mean(yc * yc, axis=-1, keepdims=True)
    yn = yc * lax.rsqrt(var + LN_EPS) * lg_ref[...] + lb_ref[...]
    sw = (yn * _sigmoid(yn)).astype(BF16)
    yb = jnp.dot(sw, wpw_ref[...], preferred_element_type=F32)

    mid = slice(HALO, HALO + CH)
    low_grp = lax.broadcasted_iota(jnp.int32, (CH, 128), 1) < POOL_GW
    wd0 = padd_ref[pl.ds(base, W), 0:128]
    a2 = wd0 + shift_up(wd0, -1)
    a4 = shift_up(a2, -1) + shift_up(a2, 1)
    wd1 = padd_ref[pl.ds(base, W), 128:256]
    b2 = wd1 + shift_up(wd1, -1)
    b4 = shift_up(b2, -1) + shift_up(b2, 1)
    b8 = shift_up(b4, -2) + shift_up(b4, 2)
    b16 = shift_up(b8, -4) + shift_up(b8, 4)
    wsum = jnp.concatenate([jnp.where(low_grp, a2[mid], a4[mid]),
                            jnp.where(low_grp, b8[mid], b16[mid])], axis=1)
    u = jnp.concatenate([wd0[mid], wd1[mid]], axis=1)
    grp = lax.broadcasted_iota(jnp.int32, (CH, BRANCH_W), 1) >> 6
    half = jnp.where(grp == 0, POOL_HALVES[0],
                     jnp.where(grp == 1, POOL_HALVES[1],
                               jnp.where(grp == 2, POOL_HALVES[2], POOL_HALVES[3])))
    pos = pos0 + lax.broadcasted_iota(jnp.int32, (CH, BRANCH_W), 0)
    lo = jnp.maximum(pos - half, 0)
    hi = jnp.minimum(pos + half - 1, S - 1)
    cnt = (hi - lo + 1).astype(F32)
    pooled = (wsum / cnt - u).astype(BF16)
    yd = jnp.dot(pooled, wpool_ref[...], preferred_element_type=F32) * ps_ref[...]
    return yb, yd


def _attn_kernel(q_ref, k_ref, v_ref, bias_ref, ones_ref, o_ref, lse_ref, s_scr, p_scr, m_scr,
                 *, L):
    TQ, TK, H = ATT_TQ, ATT_TK, HEADS_PER_GROUP
    nt = L // TQ
    n_tiles = q_ref.shape[1] * nt
    q_head = lax.broadcasted_iota(jnp.int32, (TQ, BRANCH_W), 1) >> 6
    v_head = lax.broadcasted_iota(jnp.int32, (TK, BRANCH_W), 1) >> 6

    def tile_pos(idx):
        r = idx // nt
        q0 = pl.multiple_of((idx - r * nt) * TQ, TQ)
        ks = pl.multiple_of(jnp.clip(q0 - ATT_HALF, 0, L - TK), ATT_HALF)
        return r, q0, ks

    def score_stage(idx, slot):
        r, q0, ks = tile_pos(idx)
        q = q_ref[0, r, pl.ds(q0, TQ), :]
        k = k_ref[0, r, pl.ds(ks, TK), :]
        bias = bias_ref[(q0 - ks) // ATT_HALF]
        for h in range(H):
            qh = jnp.where(q_head == h, q, jnp.zeros_like(q))
            s_scr[slot, h] = lax.dot_general(qh, k, (((1,), (1,)), ((), ())),
                                             preferred_element_type=F32) + bias

    def softmax_stage(idx, slot):
        mb = jnp.zeros((TQ, BRANCH_W), F32)
        for h in range(H):
            s = s_scr[slot, h]
            m = jnp.max(s, axis=-1, keepdims=True)
            p_scr[slot, :, h * TK:(h + 1) * TK] = jnp.exp2(s - m).astype(BF16)
            mb = jnp.where(q_head == h, m, mb)
        m_scr[slot] = mb

    def value_stage(idx, slot):
        r, q0, ks = tile_pos(idx)
        v = v_ref[0, r, pl.ds(ks, TK), :]
        vs = jnp.concatenate([jnp.where(v_head == h, v, jnp.zeros_like(v)) for h in range(H)],
                             axis=0)
        res = jnp.dot(p_scr[slot], jnp.concatenate([vs, ones_ref[...]], axis=1),
                      preferred_element_type=F32)
        den = res[:, BRANCH_W:]
        o_ref[0, r, pl.ds(q0, TQ), :] = (res[:, :BRANCH_W] / den).astype(BF16)
        lse_ref[0, r, pl.ds(q0, TQ), :] = m_scr[slot] * LN2 + jnp.log(den)

    D, NS = ATT_STAGE_GAP, ATT_SLOTS

    def step(i, a, scores=True, values=True):
        if values:
            value_stage(i - D, (a - D) % NS)
        if scores:
            score_stage(i + D, (a + D) % NS)
        softmax_stage(i, a % NS)

    for i in range(D):
        score_stage(i, i)
    for i in range(D):
        step(i, i, values=False)

    def body(j, carry):
        i0 = D + NS * j
        for a in range(NS):
            step(i0 + a, D + a)
        return carry

    steady = n_tiles - 2 * D
    assert steady % NS == 0
    lax.fori_loop(0, steady // NS, body, 0)
    for i in range(n_tiles - D, n_tiles):
        step(i, i, scores=False)
    for i in range(n_tiles - D, n_tiles):
        value_stage(i, i % NS)


def _attn_bias():
    i = np.arange(ATT_TQ)[:, None]
    j = np.arange(ATT_TK)[None, :]
    offs = np.arange(0, ATT_TK - ATT_TQ + 1, ATT_HALF)[:, None, None]
    return jnp.asarray(np.where(np.abs(i + offs - j) <= ATT_HALF, 0.0, NEG_BIG), F32)


def _attn_head_ones():
    rows = np.arange(HEADS_PER_GROUP * ATT_TK)[:, None] // ATT_TK
    lanes = np.arange(BRANCH_W)[None, :] // HEAD_DIM
    return jnp.asarray(rows == lanes, F32)


def _attn_call(q, k, v, bias, ones, B, d, L):
    blk = pl.BlockSpec((1, d, L, BRANCH_W), lambda b: (b, 0, 0, 0))
    return pl.pallas_call(
        functools.partial(_attn_kernel, L=L),
        grid=(B,),
        in_specs=[blk, blk, blk, _const_spec(bias.shape), _const_spec(ones.shape)],
        out_specs=[blk, blk],
        out_shape=[jax.ShapeDtypeStruct((B, d, L, BRANCH_W), BF16),
                   jax.ShapeDtypeStruct((B, d, L, BRANCH_W), F32)],
        scratch_shapes=[
            pltpu.VMEM((ATT_SLOTS, HEADS_PER_GROUP, ATT_TQ, ATT_TK), F32),
            pltpu.VMEM((ATT_SLOTS, ATT_TQ, HEADS_PER_GROUP * ATT_TK), BF16),
            pltpu.VMEM((ATT_SLOTS, ATT_TQ, BRANCH_W), F32)],
        compiler_params=pltpu.CompilerParams(
            dimension_semantics=("arbitrary",), vmem_limit_bytes=VMEM_LIMIT),
        name=f"attn_d{d}",
    )(q, k, v, bias, ones)


def _merge_kernel(x_ref, g_ref, f_ref, ubp_ref, ub_ref, ubn_ref, udp_ref, ud_ref, udn_ref,
                  o0_ref, l0_ref, o1_ref, l1_ref, o2_ref, l2_ref, gp_ref,
                  cw_ref, cb_ref, lg_ref, lb_ref, wpw_ref, wpool_ref, ps_ref,
                  wlin_ref, wg_ref, bg_ref, wb_ref, wo_ref, fg_ref,
                  out_ref, so1_ref, sl1_ref, so2_ref, sl2_ref, padb_ref, padd_ref, *, final, S):
    x = x_ref[...]
    h = _rms(x, g_ref[...]).astype(BF16)

    nt = S // TM
    t = pl.program_id(0) % nt
    for pad_ref, p_ref, c_ref, n_ref in ((padb_ref, ubp_ref, ub_ref, ubn_ref),
                                         (padd_ref, udp_ref, ud_ref, udn_ref)):
        pad_ref[0:LOCAL_HALO, :] = jnp.where(t > 0, p_ref[0], 0.0)
        pad_ref[LOCAL_HALO:LOCAL_HALO + TM, :] = c_ref[0]
        pad_ref[LOCAL_HALO + TM:, :] = jnp.where(t < nt - 1, n_ref[0], 0.0)

    def conv_pool(c):
        return _conv_pool_chunk(padb_ref, padd_ref, LOCAL_CH * c, t * TM + LOCAL_CH * c,
                                cw_ref, cb_ref, lg_ref, lb_ref, wpw_ref, wpool_ref, ps_ref, S)

    def interleave(src_ref, scr_ref, d):
        rows = TM // d
        for r in range(d):
            piece = src_ref[0, r].astype(F32)
            scr_ref[0, pl.ds(r, rows, stride=d), :] = piece[:, :128]
            scr_ref[1, pl.ds(r, rows, stride=d), :] = piece[:, 128:]
        return jnp.concatenate([scr_ref[0], scr_ref[1]], axis=1)

    os_ = [o0_ref[0, 0].astype(F32), interleave(o1_ref, so1_ref, DILATIONS[1]),
           interleave(o2_ref, so2_ref, DILATIONS[2])]
    ls = [l0_ref[0, 0], interleave(l1_ref, sl1_ref, DILATIONS[1]),
          interleave(l2_ref, sl2_ref, DILATIONS[2])]
    m = jnp.maximum(jnp.maximum(ls[0], ls[1]), ls[2])
    es = [jnp.exp(l - m) for l in ls]
    den = es[0] + es[1] + es[2]
    yc = (os_[0] * es[0] + os_[1] * es[1] + os_[2] * es[2]) / den

    f = jnp.concatenate(
        [jnp.concatenate([f_ref[0, half, FFT_PITCH * j:FFT_PITCH * j + FFT_R, :]
                          for j in range(TM // FFT_R)], axis=0) for half in range(2)],
        axis=1).astype(BF16)
    ya = jnp.dot(f, wlin_ref[...], preferred_element_type=F32)

    def branch(n, y):
        gate_path = gp_ref[:, BRANCH_W * n:BRANCH_W * (n + 1)].astype(F32)
        yn = jnp.dot((y * gate_path).astype(BF16), wb_ref[n], preferred_element_type=F32)
        th = jnp.tanh(jnp.dot(h, wg_ref[n], preferred_element_type=F32) + bg_ref[n])
        return th * yn + yn

    n_chunks = TM // LOCAL_CH
    merged = branch(0, ya)
    chunks = [conv_pool(c) for c in range(n_chunks // 2)]
    merged = merged + branch(2, yc)
    chunks += [conv_pool(c) for c in range(n_chunks // 2, n_chunks)]
    merged = merged + branch(1, jnp.concatenate([c[0] for c in chunks], axis=0))
    merged = merged + branch(3, jnp.concatenate([c[1] for c in chunks], axis=0))
    out = x + jnp.dot(merged.astype(BF16), wo_ref[...], preferred_element_type=F32)
    if final:
        out = _rms(out, fg_ref[...])
    out_ref[...] = out


def _merge_call(x2, g, f, ub, ud, att, gp, local_w, w_lin, w_gate, b_gate, w_branch, w_out,
                final_g, B, S, final):
    T = B * S
    nt = S // TM
    row = lambda i: (i, 0)
    att_specs = []
    for d in DILATIONS:
        spec = pl.BlockSpec((1, d, TM // d, BRANCH_W), lambda i: (i // nt, 0, i % nt, 0))
        att_specs += [spec, spec]
    tile = lambda w: pl.BlockSpec((TM, w), row)
    seq = pl.BlockSpec((1, TM, BRANCH_W), lambda i: (i // nt, i % nt, 0))
    hb = TM // LOCAL_HALO
    prev = pl.BlockSpec((1, LOCAL_HALO, BRANCH_W),
                        lambda i: (i // nt, jnp.maximum((i % nt) * hb - 1, 0), 0))
    nxt = pl.BlockSpec((1, LOCAL_HALO, BRANCH_W),
                       lambda i: (i // nt, jnp.minimum((i % nt + 1) * hb, S // LOCAL_HALO - 1), 0))
    vec = _const_spec((1, BRANCH_W))
    mat = _const_spec((BRANCH_W, BRANCH_W))
    split = pl.BlockSpec((1, 2, TM // FFT_R * FFT_PITCH, 128), lambda i: (i // nt, 0, i % nt, 0))
    return pl.pallas_call(
        functools.partial(_merge_kernel, final=final, S=S),
        grid=(T // TM,),
        in_specs=[tile(D_MODEL), _const_spec((1, D_MODEL)), split,
                  prev, seq, nxt, prev, seq, nxt] + att_specs + [
            tile(N_BRANCH * BRANCH_W),
            _const_spec((CONV_WIDTH, BRANCH_W)), vec, vec, vec, mat, mat, vec,
            _const_spec((BRANCH_W, BRANCH_W)),
            _const_spec((N_BRANCH, D_MODEL, D_MODEL)),
            _const_spec((N_BRANCH, 1, D_MODEL)),
            _const_spec((N_BRANCH, BRANCH_W, D_MODEL)),
            _const_spec((D_MODEL, D_MODEL)),
            _const_spec((1, D_MODEL)),
        ],
        out_specs=tile(D_MODEL),
        out_shape=jax.ShapeDtypeStruct((T, D_MODEL), F32),
        scratch_shapes=[pltpu.VMEM((2, TM, 128), F32)] * 4
        + [pltpu.VMEM((TM + 2 * LOCAL_HALO, BRANCH_W), F32)] * 2,
        compiler_params=pltpu.CompilerParams(
            dimension_semantics=("arbitrary",), vmem_limit_bytes=VMEM_LIMIT),
        name="merge",
    )(x2, g, f, ub, ub, ub, ud, ud, ud, *att, gp, *local_w, w_lin, w_gate, b_gate, w_branch,
      w_out, final_g)


def _dft_constants():
    n = np.arange(FFT_R)
    ang = 2.0 * np.pi * np.outer(n, n) / FFT_R
    a, b = np.cos(ang), np.sin(ang)
    eye = np.eye(BRANCH_W // FOURIER_GW)
    dft_c = np.concatenate([np.kron(eye, a), np.kron(eye, b)], axis=1)
    scale = 1.0 / np.sqrt(FFT_R * FFT_R * FOURIER_GW)
    m1 = np.block([[a, -b], [-b, -a]]) * scale
    m3 = np.concatenate([a, b], axis=1)
    tw = 2.0 * np.pi * np.outer(n, n) / (FFT_R * FFT_R)
    return tuple(jnp.asarray(a, F32) for a in (dft_c, m1, m3, np.cos(tw), -np.sin(tw)))


def _rope_tables(S):
    inv = 1.0 / (ROPE_THETA ** (jnp.arange(0, HEAD_DIM, 2, dtype=F32) / HEAD_DIM))
    ang = jnp.arange(S, dtype=F32)[:, None] * inv[None, :]
    cos, sin = jnp.cos(ang), jnp.sin(ang)
    cos_t = jnp.concatenate([cos, cos, cos, cos], axis=1)
    sin_t = jnp.concatenate([-sin, sin, -sin, sin], axis=1)
    return cos_t, sin_t


def kernel(x, norm_g, w_in, w_fourier, conv_w, conv_b, conv_ln_g, conv_ln_b, w_pw, w_pool,
           pool_scale, w_branch, w_gate, b_gate, w_out, final_g):
    B, S, D = x.shape
    depth = norm_g.shape[0]
    assert D == D_MODEL and S == FFT_R * FFT_R and S % TM_PROJ == 0 and w_in.shape[-1] == D_IN
    dft_c, m1, m3, tw_r, tw_i = _dft_constants()
    dft_c, m1, m3 = dft_c.astype(BF16), m1.astype(BF16), m3.astype(BF16)
    tr = jnp.broadcast_to(tw_r.T[:, :, None], (FFT_R, FFT_R, 128))
    ti = jnp.broadcast_to(tw_i.T[:, :, None], (FFT_R, FFT_R, 128))
    cos_t, sin_t = _rope_tables(S)
    att_bias = _attn_bias()
    att_ones = _attn_head_ones().astype(BF16)
    row = lambda a: a.reshape(1, -1)

    x2 = x.reshape(B * S, D)
    for l in range(depth):
        ps, qs, ub, ud, gp, *qkv = _proj_call(
            x2, row(norm_g[l]), w_in[l].astype(BF16), dft_c, cos_t, sin_t, B, S)
        f = _dft_call(ps, qs, m1, m3, tr, ti, B, S)
        w_pool_bd = jax.scipy.linalg.block_diag(*[w_pool[l, g] for g in range(w_pool.shape[1])])
        local_w = (conv_w[l], row(conv_b[l]), row(conv_ln_g[l]), row(conv_ln_b[l]),
                   w_pw[l].astype(BF16), w_pool_bd.astype(BF16), row(pool_scale[l]))
        att = []
        for g, d in enumerate(DILATIONS):
            att += _attn_call(*qkv[3 * g:3 * g + 3], att_bias, att_ones, B, d, S // d)
        x2 = _merge_call(
            x2, row(norm_g[l]), f, ub, ud, att,
            gp, local_w, w_fourier[l].astype(BF16), (0.5 * w_gate[l]).astype(BF16),
            0.5 * b_gate[l].reshape(N_BRANCH, 1, D_MODEL), w_branch[l].astype(BF16),
            (0.5 * w_out[l]).astype(BF16), row(final_g), B, S, final=(l == depth - 1))
    return x2.reshape(B, S, D)
```

```python
import functools

import numpy as np
import jax
import jax.numpy as jnp
from jax import lax
from jax.experimental import pallas as pl
from jax.experimental.pallas import tpu as pltpu

F32 = jnp.float32
BF16 = jnp.bfloat16

D_MODEL = 1024
N_BRANCH = 4
BRANCH_W = 256
FOURIER_GW = 64
CONV_WIDTH = 31
CONV_PAD = CONV_WIDTH // 2
LN_EPS = 1e-5
HEAD_DIM = 64
HEADS_PER_GROUP = 4
DILATIONS = (1, 4, 16)
ATT_HALF = 64
ATT_QKV_W = 768
ROPE_THETA = 10000.0
NEG_BIG = -1e30
POOL_HALVES = (1, 2, 4, 8)
POOL_GW = 64
NORM_EPS = 1e-6

COL_A = 0
COL_B = 256
COL_Q = 768
COL_K = COL_Q + ATT_QKV_W
COL_V = COL_K + ATT_QKV_W
COL_D = COL_V + ATT_QKV_W
COL_GATE = COL_D + BRANCH_W
D_IN = COL_GATE + N_BRANCH * BRANCH_W

FFT_R = 64
FFT_PITCH = 72
TM_PROJ = 1024
TM = 512
LOG2E = 1.4426950408889634
LN2 = 0.6931471805599453
ATT_TQ = 128
ATT_TK = ATT_TQ + 2 * ATT_HALF
ATT_STAGE_GAP = 4
ATT_SLOTS = 2 * ATT_STAGE_GAP
LOCAL_CH = 64
LOCAL_HALO = 16
VMEM_LIMIT = 56 * 1024 * 1024


def _sigmoid(x):
    return 0.5 * jnp.tanh(0.5 * x) + 0.5


def _rms(x, g):
    ms = jnp.mean(x * x, axis=-1, keepdims=True)
    return x * lax.rsqrt(ms + NORM_EPS) * g


def _const_spec(shape):
    nd = len(shape)
    return pl.BlockSpec(shape, lambda *_: (0,) * nd, pipeline_mode=pl.Buffered(1))


def _proj_kernel(x_ref, g_ref, w_ref, dft_ref, cos_ref, sin_ref,
                 ps_ref, qs_ref, ub_ref, ud_ref, gp_ref,
                 q0_ref, k0_ref, v0_ref, q1_ref, k1_ref, v1_ref, q2_ref, k2_ref, v2_ref,
                 scr_ref):
    tm = x_ref.shape[0]
    h = _rms(x_ref[...], g_ref[...]).astype(BF16)

    def proj(a, b):
        return jnp.dot(h, w_ref[:, a:b], preferred_element_type=F32)

    ua = proj(COL_A, COL_B).astype(BF16)
    pq = jnp.dot(ua, dft_ref[...], preferred_element_type=F32)
    for dst_ref, col in ((ps_ref, 0), (qs_ref, BRANCH_W)):
        rows = tm // FFT_R
        for n1 in range(rows):
            src = slice(FFT_R * n1, FFT_R * (n1 + 1))
            dst = slice(FFT_PITCH * n1, FFT_PITCH * n1 + FFT_R)
            scr_ref[0, dst, :] = pq[src, col:col + 128]
            scr_ref[1, dst, :] = pq[src, col + 128:col + 256]
        for n2 in range(FFT_R):
            dst_ref[0, n2] = jnp.concatenate(
                [scr_ref[0, pl.ds(n2, rows, stride=FFT_PITCH), :],
                 scr_ref[1, pl.ds(n2, rows, stride=FFT_PITCH), :]], axis=1).astype(BF16)
    zb = proj(COL_B, COL_Q)
    ub_ref[0] = zb[:, :BRANCH_W] * _sigmoid(zb[:, BRANCH_W:])

    cos = cos_ref[...]
    sin = sin_ref[...]
    lane = lax.broadcasted_iota(jnp.int32, cos.shape, 1)
    first_half = (lane & (HEAD_DIM - 1)) < HEAD_DIM // 2

    def rope(t):
        outs = []
        for j in range(ATT_QKV_W // 128):
            c = t[:, 128 * j:128 * (j + 1)]
            rot = jnp.where(first_half, pltpu.roll(c, 128 - HEAD_DIM // 2, 1),
                            pltpu.roll(c, HEAD_DIM // 2, 1))
            outs.append(c * cos + rot * sin)
        return outs

    def split_store(chunks, refs):
        for g, d in enumerate(DILATIONS):
            if d == 1:
                refs[g][0, 0] = jnp.concatenate(chunks[2 * g:2 * g + 2], axis=1).astype(BF16)
            else:
                scr_ref[0, 0:tm, :] = chunks[2 * g]
                scr_ref[1, 0:tm, :] = chunks[2 * g + 1]
                rows = tm // d
                for r in range(d):
                    refs[g][0, r] = jnp.concatenate(
                        [scr_ref[0, pl.ds(r, rows, stride=d), :],
                         scr_ref[1, pl.ds(r, rows, stride=d), :]], axis=1).astype(BF16)

    q = [c * (HEAD_DIM ** -0.5 * LOG2E) for c in rope(proj(COL_Q, COL_K))]
    split_store(q, (q0_ref, q1_ref, q2_ref))
    split_store(rope(proj(COL_K, COL_V)), (k0_ref, k1_ref, k2_ref))
    v = proj(COL_V, COL_D)
    split_store([v[:, 128 * j:128 * (j + 1)] for j in range(ATT_QKV_W // 128)],
                (v0_ref, v1_ref, v2_ref))
    zg = proj(COL_GATE, D_IN)
    gp_ref[...] = (zg * _sigmoid(zg)).astype(BF16)
    ud_ref[0] = proj(COL_D, COL_GATE)


def _proj_call(x2, g, w_in, dft_c, cos_t, sin_t, B, S):
    T = B * S
    tm = TM_PROJ
    nt = S // tm
    row = lambda i: (i, 0)
    qkv_shapes, qkv_specs = [], []
    for d in DILATIONS:
        for _ in range(3):
            qkv_shapes.append(jax.ShapeDtypeStruct((B, d, S // d, BRANCH_W), BF16))
            qkv_specs.append(pl.BlockSpec((1, d, tm // d, BRANCH_W),
                                          lambda i: (i // nt, 0, i % nt, 0)))
    slab_shape = jax.ShapeDtypeStruct((B, FFT_R, S // FFT_R, BRANCH_W), BF16)
    slab_spec = pl.BlockSpec((1, FFT_R, tm // FFT_R, BRANCH_W), lambda i: (i // nt, 0, i % nt, 0))
    seq_spec = pl.BlockSpec((1, tm, BRANCH_W), lambda i: (i // nt, i % nt, 0))
    out_shape = [
        slab_shape, slab_shape,
        jax.ShapeDtypeStruct((B, S, BRANCH_W), F32),
        jax.ShapeDtypeStruct((B, S, BRANCH_W), F32),
        jax.ShapeDtypeStruct((T, N_BRANCH * BRANCH_W), BF16),
    ] + qkv_shapes
    out_specs = [
        slab_spec, slab_spec, seq_spec, seq_spec,
        pl.BlockSpec((tm, N_BRANCH * BRANCH_W), row),
    ] + qkv_specs
    return pl.pallas_call(
        _proj_kernel,
        grid=(T // tm,),
        in_specs=[
            pl.BlockSpec((tm, D_MODEL), row),
            _const_spec((1, D_MODEL)),
            _const_spec((D_MODEL, D_IN)),
            _const_spec((BRANCH_W, 2 * BRANCH_W)),
            pl.BlockSpec((tm, 128), lambda i: (i % nt, 0)),
            pl.BlockSpec((tm, 128), lambda i: (i % nt, 0)),
        ],
        out_specs=out_specs,
        out_shape=out_shape,
        scratch_shapes=[pltpu.VMEM((2, tm // FFT_R * FFT_PITCH, 128), F32)],
        compiler_params=pltpu.CompilerParams(
            dimension_semantics=("arbitrary",), vmem_limit_bytes=VMEM_LIMIT),
        name="proj",
    )(x2, g, w_in, dft_c, cos_t, sin_t)


def _dft_kernel(ps_ref, qs_ref, m1_ref, m3_ref, tr_ref, ti_ref, out_ref, zr_scr, zi_scr):
    R, PITCH = FFT_R, FFT_PITCH
    m1 = m1_ref[...]
    m3 = m3_ref[...]

    def stage1(n2, carry):
        x = jnp.concatenate([ps_ref[0, n2], qs_ref[0, n2]], axis=0)
        y = jnp.dot(m1, x, preferred_element_type=F32)
        tr = tr_ref[n2]
        ti = ti_ref[n2]
        dst = pl.ds(pl.multiple_of(n2 * PITCH, 8), R)
        for half in range(2):
            lanes = slice(128 * half, 128 * (half + 1))
            yr, yi = y[:R, lanes], y[R:, lanes]
            zr_scr[half, dst, :] = yr * tr - yi * ti
            zi_scr[half, dst, :] = yr * ti + yi * tr
        return carry

    lax.fori_loop(0, R, stage1, 0, unroll=32)

    def zero_pad(k2, carry):
        pad = pl.ds(pl.multiple_of(k2 * PITCH + R, 8), PITCH - R)
        out_ref[0, 0, pad, :] = jnp.zeros((PITCH - R, 128), F32)
        out_ref[0, 1, pad, :] = jnp.zeros((PITCH - R, 128), F32)
        return carry

    lax.fori_loop(0, R, zero_pad, 0, unroll=8)

    def stage3(k1, carry):
        rows = pl.ds(k1, R, stride=PITCH)
        z = jnp.concatenate(
            [jnp.concatenate([zr_scr[0, rows, :], zr_scr[1, rows, :]], axis=1),
             jnp.concatenate([zi_scr[0, rows, :], zi_scr[1, rows, :]], axis=1)], axis=0)
        o = jnp.dot(m3, z.astype(BF16), preferred_element_type=F32)
        out_ref[0, 0, rows, :] = o[:, :128]
        out_ref[0, 1, rows, :] = o[:, 128:]
        return carry

    lax.fori_loop(0, R, stage3, 0, unroll=32)


def _dft_call(ps, qs, m1, m3, tr, ti, B, S):
    padded = FFT_R * FFT_PITCH
    slab = pl.BlockSpec((1, FFT_R, FFT_R, BRANCH_W), lambda b: (b, 0, 0, 0))
    return pl.pallas_call(
        _dft_kernel,
        grid=(B,),
        in_specs=[slab, slab, _const_spec((2 * FFT_R, 2 * FFT_R)), _const_spec((FFT_R, 2 * FFT_R)),
                  _const_spec((FFT_R, FFT_R, 128)), _const_spec((FFT_R, FFT_R, 128))],
        out_specs=pl.BlockSpec((1, 2, padded, 128), lambda b: (b, 0, 0, 0)),
        out_shape=jax.ShapeDtypeStruct((B, 2, padded, 128), F32),
        scratch_shapes=[pltpu.VMEM((2, padded, 128), F32)] * 2,
        compiler_params=pltpu.CompilerParams(
            dimension_semantics=("arbitrary",), vmem_limit_bytes=VMEM_LIMIT),
        name="dft",
    )(ps, qs, m1, m3, tr, ti)


def _conv_pool_chunk(padb_ref, padd_ref, base, pos0, cw_ref, cb_ref, lg_ref, lb_ref, wpw_ref,
                     wpool_ref, ps_ref, S):
    CH, HALO = LOCAL_CH, LOCAL_HALO
    W = CH + 2 * HALO

    def shift_up(x, s):
        return pltpu.roll(x, (W - s) % W, 0)

    accs = []
    for lanes in (slice(0, 128), slice(128, 256)):
        win = padb_ref[pl.ds(base, W), lanes]
        acc = jnp.zeros((CH, 128), F32)
        for r in range(8):
            wr = win if r == 0 else shift_up(win, r)
            for a in range(W // 8):
                k = 8 * a + r - (HALO - CONV_PAD)
                if 0 <= k < CONV_WIDTH:
                    acc = acc + wr[8 * a:8 * a + CH] * cw_ref[pl.ds(k, 1), :][:, lanes]
        accs.append(acc)
    y = jnp.concatenate(accs, axis=1) + cb_ref[...]
    mu = jnp.mean(y, axis=-1, keepdims=True)
    yc = y - mu
    var = jnp.mean(yc * yc, axis=-1, keepdims=True)
    yn = yc * lax.rsqrt(var + LN_EPS) * lg_ref[...] + lb_ref[...]
    sw = (yn * _sigmoid(yn)).astype(BF16)
    yb = jnp.dot(sw, wpw_ref[...], preferred_element_type=F32)

    mid = slice(HALO, HALO + CH)
    low_grp = lax.broadcasted_iota(jnp.int32, (CH, 128), 1) < POOL_GW
    wd0 = padd_ref[pl.ds(base, W), 0:128]
    a2 = wd0 + shift_up(wd0, -1)
    a4 = shift_up(a2, -1) + shift_up(a2, 1)
    wd1 = padd_ref[pl.ds(base, W), 128:256]
    b2 = wd1 + shift_up(wd1, -1)
    b4 = shift_up(b2, -1) + shift_up(b2, 1)
    b8 = shift_up(b4, -2) + shift_up(b4, 2)
    b16 = shift_up(b8, -4) + shift_up(b8, 4)
    wsum = jnp.concatenate([jnp.where(low_grp, a2[mid], a4[mid]),
                            jnp.where(low_grp, b8[mid], b16[mid])], axis=1)
    u = jnp.concatenate([wd0[mid], wd1[mid]], axis=1)
    grp = lax.broadcasted_iota(jnp.int32, (CH, BRANCH_W), 1) >> 6
    half = jnp.where(grp == 0, POOL_HALVES[0],
                     jnp.where(grp == 1, POOL_HALVES[1],
                               jnp.where(grp == 2, POOL_HALVES[2], POOL_HALVES[3])))
    pos = pos0 + lax.broadcasted_iota(jnp.int32, (CH, BRANCH_W), 0)
    lo = jnp.maximum(pos - half, 0)
    hi = jnp.minimum(pos + half - 1, S - 1)
    cnt = (hi - lo + 1).astype(F32)
    pooled = (wsum / cnt - u).astype(BF16)
    yd = jnp.dot(pooled, wpool_ref[...], preferred_element_type=F32) * ps_ref[...]
    return yb, yd


def _attn_kernel(q_ref, k_ref, v_ref, bias_ref, ones_ref, o_ref, lse_ref, s_scr, p_scr, m_scr,
                 *, L):
    TQ, TK, H = ATT_TQ, ATT_TK, HEADS_PER_GROUP
    nt = L // TQ
    n_tiles = q_ref.shape[1] * nt
    q_head = lax.broadcasted_iota(jnp.int32, (TQ, BRANCH_W), 1) >> 6
    v_head = lax.broadcasted_iota(jnp.int32, (TK, BRANCH_W), 1) >> 6

    def tile_pos(idx):
        r = idx // nt
        q0 = pl.multiple_of((idx - r * nt) * TQ, TQ)
        ks = pl.multiple_of(jnp.clip(q0 - ATT_HALF, 0, L - TK), ATT_HALF)
        return r, q0, ks

    def score_stage(idx, slot):
        r, q0, ks = tile_pos(idx)
        q = q_ref[0, r, pl.ds(q0, TQ), :]
        k = k_ref[0, r, pl.ds(ks, TK), :]
        bias = bias_ref[(q0 - ks) // ATT_HALF]
        for h in range(H):
            qh = jnp.where(q_head == h, q, jnp.zeros_like(q))
            s_scr[slot, h] = lax.dot_general(qh, k, (((1,), (1,)), ((), ())),
                                             preferred_element_type=F32) + bias

    def softmax_stage(idx, slot):
        mb = jnp.zeros((TQ, BRANCH_W), F32)
        for h in range(H):
            s = s_scr[slot, h]
            m = jnp.max(s, axis=-1, keepdims=True)
            p_scr[slot, :, h * TK:(h + 1) * TK] = jnp.exp2(s - m).astype(BF16)
            mb = jnp.where(q_head == h, m, mb)
        m_scr[slot] = mb

    def value_stage(idx, slot):
        r, q0, ks = tile_pos(idx)
        v = v_ref[0, r, pl.ds(ks, TK), :]
        vs = jnp.concatenate([jnp.where(v_head == h, v, jnp.zeros_like(v)) for h in range(H)],
                             axis=0)
        res = jnp.dot(p_scr[slot], jnp.concatenate([vs, ones_ref[...]], axis=1),
                      preferred_element_type=F32)
        den = res[:, BRANCH_W:]
        o_ref[0, r, pl.ds(q0, TQ), :] = (res[:, :BRANCH_W] / den).astype(BF16)
        lse_ref[0, r, pl.ds(q0, TQ), :] = m_scr[slot] * LN2 + jnp.log(den)

    D, NS = ATT_STAGE_GAP, ATT_SLOTS

    def step(i, a, scores=True, values=True):
        if values:
            value_stage(i - D, (a - D) % NS)
        if scores:
            score_stage(i + D, (a + D) % NS)
        softmax_stage(i, a % NS)

    for i in range(D):
        score_stage(i, i)
    for i in range(D):
        step(i, i, values=False)

    def body(j, carry):
        i0 = D + NS * j
        for a in range(NS):
            step(i0 + a, D + a)
        return carry

    steady = n_tiles - 2 * D
    assert steady % NS == 0
    lax.fori_loop(0, steady // NS, body, 0)
    for i in range(n_tiles - D, n_tiles):
        step(i, i, scores=False)
    for i in range(n_tiles - D, n_tiles):
        value_stage(i, i % NS)


def _attn_bias():
    i = np.arange(ATT_TQ)[:, None]
    j = np.arange(ATT_TK)[None, :]
    offs = np.arange(0, ATT_TK - ATT_TQ + 1, ATT_HALF)[:, None, None]
    return jnp.asarray(np.where(np.abs(i + offs - j) <= ATT_HALF, 0.0, NEG_BIG), F32)


def _attn_head_ones():
    rows = np.arange(HEADS_PER_GROUP * ATT_TK)[:, None] // ATT_TK
    lanes = np.arange(BRANCH_W)[None, :] // HEAD_DIM
    return jnp.asarray(rows == lanes, F32)


def _attn_call(q, k, v, bias, ones, B, d, L):
    blk = pl.BlockSpec((1, d, L, BRANCH_W), lambda b: (b, 0, 0, 0))
    return pl.pallas_call(
        functools.partial(_attn_kernel, L=L),
        grid=(B,),
        in_specs=[blk, blk, blk, _const_spec(bias.shape), _const_spec(ones.shape)],
        out_specs=[blk, blk],
        out_shape=[jax.ShapeDtypeStruct((B, d, L, BRANCH_W), BF16),
                   jax.ShapeDtypeStruct((B, d, L, BRANCH_W), F32)],
        scratch_shapes=[
            pltpu.VMEM((ATT_SLOTS, HEADS_PER_GROUP, ATT_TQ, ATT_TK), F32),
            pltpu.VMEM((ATT_SLOTS, ATT_TQ, HEADS_PER_GROUP * ATT_TK), BF16),
            pltpu.VMEM((ATT_SLOTS, ATT_TQ, BRANCH_W), F32)],
        compiler_params=pltpu.CompilerParams(
            dimension_semantics=("arbitrary",), vmem_limit_bytes=VMEM_LIMIT),
        name=f"attn_d{d}",
    )(q, k, v, bias, ones)


def _merge_kernel(x_ref, g_ref, f_ref, ubp_ref, ub_ref, ubn_ref, udp_ref, ud_ref, udn_ref,
                  o0_ref, l0_ref, o1_ref, l1_ref, o2_ref, l2_ref, gp_ref,
                  cw_ref, cb_ref, lg_ref, lb_ref, wpw_ref, wpool_ref, ps_ref,
                  wlin_ref, wg_ref, bg_ref, wb_ref, wo_ref, fg_ref,
                  out_ref, so1_ref, sl1_ref, so2_ref, sl2_ref, padb_ref, padd_ref, *, final, S):
    x = x_ref[...]
    h = _rms(x, g_ref[...]).astype(BF16)

    nt = S // TM
    t = pl.program_id(0) % nt
    for pad_ref, p_ref, c_ref, n_ref in ((padb_ref, ubp_ref, ub_ref, ubn_ref),
                                         (padd_ref, udp_ref, ud_ref, udn_ref)):
        pad_ref[0:LOCAL_HALO, :] = jnp.where(t > 0, p_ref[0], 0.0)
        pad_ref[LOCAL_HALO:LOCAL_HALO + TM, :] = c_ref[0]
        pad_ref[LOCAL_HALO + TM:, :] = jnp.where(t < nt - 1, n_ref[0], 0.0)

    def conv_pool(c):
        return _conv_pool_chunk(padb_ref, padd_ref, LOCAL_CH * c, t * TM + LOCAL_CH * c,
                                cw_ref, cb_ref, lg_ref, lb_ref, wpw_ref, wpool_ref, ps_ref, S)

    def interleave(src_ref, scr_ref, d):
        rows = TM // d
        for r in range(d):
            piece = src_ref[0, r].astype(F32)
            scr_ref[0, pl.ds(r, rows, stride=d), :] = piece[:, :128]
            scr_ref[1, pl.ds(r, rows, stride=d), :] = piece[:, 128:]
        return jnp.concatenate([scr_ref[0], scr_ref[1]], axis=1)

    os_ = [o0_ref[0, 0].astype(F32), interleave(o1_ref, so1_ref, DILATIONS[1]),
           interleave(o2_ref, so2_ref, DILATIONS[2])]
    ls = [l0_ref[0, 0], interleave(l1_ref, sl1_ref, DILATIONS[1]),
          interleave(l2_ref, sl2_ref, DILATIONS[2])]
    m = jnp.maximum(jnp.maximum(ls[0], ls[1]), ls[2])
    es = [jnp.exp(l - m) for l in ls]
    den = es[0] + es[1] + es[2]
    yc = (os_[0] * es[0] + os_[1] * es[1] + os_[2] * es[2]) / den

    f = jnp.concatenate(
        [jnp.concatenate([f_ref[0, half, FFT_PITCH * j:FFT_PITCH * j + FFT_R, :]
                          for j in range(TM // FFT_R)], axis=0) for half in range(2)],
        axis=1).astype(BF16)
    ya = jnp.dot(f, wlin_ref[...], preferred_element_type=F32)

    def branch(n, y):
        gate_path = gp_ref[:, BRANCH_W * n:BRANCH_W * (n + 1)].astype(F32)
        yn = jnp.dot((y * gate_path).astype(BF16), wb_ref[n], preferred_element_type=F32)
        th = jnp.tanh(jnp.dot(h, wg_ref[n], preferred_element_type=F32) + bg_ref[n])
        return th * yn + yn

    n_chunks = TM // LOCAL_CH
    merged = branch(0, ya)
    chunks = [conv_pool(c) for c in range(n_chunks // 2)]
    merged = merged + branch(2, yc)
    chunks += [conv_pool(c) for c in range(n_chunks // 2, n_chunks)]
    merged = merged + branch(1, jnp.concatenate([c[0] for c in chunks], axis=0))
    merged = merged + branch(3, jnp.concatenate([c[1] for c in chunks], axis=0))
    out = x + jnp.dot(merged.astype(BF16), wo_ref[...], preferred_element_type=F32)
    if final:
        out = _rms(out, fg_ref[...])
    out_ref[...] = out


def _merge_call(x2, g, f, ub, ud, att, gp, local_w, w_lin, w_gate, b_gate, w_branch, w_out,
                final_g, B, S, final):
    T = B * S
    nt = S // TM
    row = lambda i: (i, 0)
    att_specs = []
    for d in DILATIONS:
        spec = pl.BlockSpec((1, d, TM // d, BRANCH_W), lambda i: (i // nt, 0, i % nt, 0))
        att_specs += [spec, spec]
    tile = lambda w: pl.BlockSpec((TM, w), row)
    seq = pl.BlockSpec((1, TM, BRANCH_W), lambda i: (i // nt, i % nt, 0))
    hb = TM // LOCAL_HALO
    prev = pl.BlockSpec((1, LOCAL_HALO, BRANCH_W),
                        lambda i: (i // nt, jnp.maximum((i % nt) * hb - 1, 0), 0))
    nxt = pl.BlockSpec((1, LOCAL_HALO, BRANCH_W),
                       lambda i: (i // nt, jnp.minimum((i % nt + 1) * hb, S // LOCAL_HALO - 1), 0))
    vec = _const_spec((1, BRANCH_W))
    mat = _const_spec((BRANCH_W, BRANCH_W))
    split = pl.BlockSpec((1, 2, TM // FFT_R * FFT_PITCH, 128), lambda i: (i // nt, 0, i % nt, 0))
    return pl.pallas_call(
        functools.partial(_merge_kernel, final=final, S=S),
        grid=(T // TM,),
        in_specs=[tile(D_MODEL), _const_spec((1, D_MODEL)), split,
                  prev, seq, nxt, prev, seq, nxt] + att_specs + [
            tile(N_BRANCH * BRANCH_W),
            _const_spec((CONV_WIDTH, BRANCH_W)), vec, vec, vec, mat, mat, vec,
            _const_spec((BRANCH_W, BRANCH_W)),
            _const_spec((N_BRANCH, D_MODEL, D_MODEL)),
            _const_spec((N_BRANCH, 1, D_MODEL)),
            _const_spec((N_BRANCH, BRANCH_W, D_MODEL)),
            _const_spec((D_MODEL, D_MODEL)),
            _const_spec((1, D_MODEL)),
        ],
        out_specs=tile(D_MODEL),
        out_shape=jax.ShapeDtypeStruct((T, D_MODEL), F32),
        scratch_shapes=[pltpu.VMEM((2, TM, 128), F32)] * 4
        + [pltpu.VMEM((TM + 2 * LOCAL_HALO, BRANCH_W), F32)] * 2,
        compiler_params=pltpu.CompilerParams(
            dimension_semantics=("arbitrary",), vmem_limit_bytes=VMEM_LIMIT),
        name="merge",
    )(x2, g, f, ub, ub, ub, ud, ud, ud, *att, gp, *local_w, w_lin, w_gate, b_gate, w_branch,
      w_out, final_g)


def _dft_constants():
    n = np.arange(FFT_R)
    ang = 2.0 * np.pi * np.outer(n, n) / FFT_R
    a, b = np.cos(ang), np.sin(ang)
    eye = np.eye(BRANCH_W // FOURIER_GW)
    dft_c = np.concatenate([np.kron(eye, a), np.kron(eye, b)], axis=1)
    scale = 1.0 / np.sqrt(FFT_R * FFT_R * FOURIER_GW)
    m1 = np.block([[a, -b], [-b, -a]]) * scale
    m3 = np.concatenate([a, b], axis=1)
    tw = 2.0 * np.pi * np.outer(n, n) / (FFT_R * FFT_R)
    return tuple(jnp.asarray(a, F32) for a in (dft_c, m1, m3, np.cos(tw), -np.sin(tw)))


def _rope_tables(S):
    inv = 1.0 / (ROPE_THETA ** (jnp.arange(0, HEAD_DIM, 2, dtype=F32) / HEAD_DIM))
    ang = jnp.arange(S, dtype=F32)[:, None] * inv[None, :]
    cos, sin = jnp.cos(ang), jnp.sin(ang)
    cos_t = jnp.concatenate([cos, cos, cos, cos], axis=1)
    sin_t = jnp.concatenate([-sin, sin, -sin, sin], axis=1)
    return cos_t, sin_t


def kernel(x, norm_g, w_in, w_fourier, conv_w, conv_b, conv_ln_g, conv_ln_b, w_pw, w_pool,
           pool_scale, w_branch, w_gate, b_gate, w_out, final_g):
    B, S, D = x.shape
    depth = norm_g.shape[0]
    assert D == D_MODEL and S == FFT_R * FFT_R and S % TM_PROJ == 0 and w_in.shape[-1] == D_IN
    dft_c, m1, m3, tw_r, tw_i = _dft_constants()
    dft_c, m1, m3 = dft_c.astype(BF16), m1.astype(BF16), m3.astype(BF16)
    tr = jnp.broadcast_to(tw_r.T[:, :, None], (FFT_R, FFT_R, 128))
    ti = jnp.broadcast_to(tw_i.T[:, :, None], (FFT_R, FFT_R, 128))
    cos_t, sin_t = _rope_tables(S)
    att_bias = _attn_bias()
    att_ones = _attn_head_ones().astype(BF16)
    row = lambda a: a.reshape(1, -1)

    x2 = x.reshape(B * S, D)
    for l in range(depth):
        ps, qs, ub, ud, gp, *qkv = _proj_call(
            x2, row(norm_g[l]), w_in[l].astype(BF16), dft_c, cos_t, sin_t, B, S)
        f = _dft_call(ps, qs, m1, m3, tr, ti, B, S)
        w_pool_bd = jax.scipy.linalg.block_diag(*[w_pool[l, g] for g in range(w_pool.shape[1])])
        local_w = (conv_w[l], row(conv_b[l]), row(conv_ln_g[l]), row(conv_ln_b[l]),
                   w_pw[l].astype(BF16), w_pool_bd.astype(BF16), row(pool_scale[l]))
        att = []
        for g, d in enumerate(DILATIONS):
            att += _attn_call(*qkv[3 * g:3 * g + 3], att_bias, att_ones, B, d, S // d)
        x2 = _merge_call(
            x2, row(norm_g[l]), f, ub, ud, att,
            gp, local_w, w_fourier[l].astype(BF16), (0.5 * w_gate[l]).astype(BF16),
            0.5 * b_gate[l].reshape(N_BRANCH, 1, D_MODEL), w_branch[l].astype(BF16),
            (0.5 * w_out[l]).astype(BF16), row(final_g), B, S, final=(l == depth - 1))
    return x2.reshape(B, S, D)
```
